```python
import math
import jax
import jax.numpy as jnp
from jax import lax
import numpy as np

D_MODEL = 1024
BATCH = 2
SEQ = 8192
DEPTH = 2
DEC_BATCH = 128
DEC_SEQ = 8
PAST_LEN = 2048
PAGE_SIZE = 128

N_META = 16
HEAD_DIM = 64
D_ATT = D_MODEL // 2
N_HEADS = D_ATT // HEAD_DIM
D_SSM = D_MODEL // 4
SSM_GROUP = 16
N_SSM_GROUPS = D_SSM // SSM_GROUP
SSM_STATE = 64
D_POOL = D_MODEL // 4
POOL_WINDOWS = (2, 4, 8, 16)
POOL_GROUP = D_POOL // len(POOL_WINDOWS)
POOL_BUF = max(POOL_WINDOWS) - 1
D_IN = 3 * D_ATT + N_HEADS + D_SSM + D_POOL
Q_BLOCK = 128
SCALE = HEAD_DIM ** -0.5
N_EXPERTS = 16
N_EXPERT_GROUPS = 4
EXPERTS_PER_GROUP = N_EXPERTS // N_EXPERT_GROUPS
TOP_K = 2
D_EXPERT = D_MODEL // 4
ALPHA = (2 * DEPTH) ** 0.25
BETA = (8 * DEPTH) ** -0.25
LN_EPS = 1e-5
NEG_INF = -1e30
DT_MIN = 1e-3
DT_MAX = 1e-1

kernel_name = 'hybrid_fox_s5_pool_moe_step'


def layer_norm(x, g, b):
    xf = x.astype(jnp.float32)
    mu = xf.mean(-1, keepdims=True)
    var = jnp.square(xf - mu).mean(-1, keepdims=True)
    y = (xf - mu) * lax.rsqrt(var + LN_EPS) * g.astype(jnp.float32) + b.astype(jnp.float32)
    return y.astype(x.dtype)


def project(x, w_in, b_f):
    z = x @ w_in
    o1, o2, o3 = D_ATT, 2 * D_ATT, 3 * D_ATT
    o4 = o3 + N_HEADS
    o5 = o4 + D_SSM
    hs = x.shape[:-1] + (N_HEADS, HEAD_DIM)
    q = z[..., :o1].reshape(hs)
    k = z[..., o1:o2].reshape(hs)
    v = z[..., o2:o3].reshape(hs)
    logf = jax.nn.log_sigmoid(z[..., o3:o4].astype(jnp.float32) + b_f.astype(jnp.float32))
    u = z[..., o4:o5].reshape(x.shape[:-1] + (N_SSM_GROUPS, SSM_GROUP))
    p = z[..., o5:]
    return q, k, v, logf, u, p


def fox_prompt(q, k, v, logf):
    B, L = q.shape[0], q.shape[1]
    lead = (-N_META) % Q_BLOCK
    tail = (-(lead + L)) % Q_BLOCK
    Lt = lead + L + tail
    pw = ((0, 0), (lead, tail), (0, 0), (0, 0))
    qp, kp, vp = jnp.pad(q, pw), jnp.pad(k, pw), jnp.pad(v, pw)
    cum = jnp.cumsum(jnp.pad(logf, pw[:3]), axis=1)
    cum_k = cum.transpose(0, 2, 1)[:, :, None, :]
    kpos = jnp.arange(Lt)
    kvalid = (kpos >= lead) & (kpos < lead + L)
    nb = Lt // Q_BLOCK
    qb = qp.reshape(B, nb, Q_BLOCK, N_HEADS, HEAD_DIM).swapaxes(0, 1)
    cb = cum.reshape(B, nb, Q_BLOCK, N_HEADS).swapaxes(0, 1)

    def block(args):
        i, qi, ci = args
        s = jnp.einsum('bqhd,bkhd->bhqk', qi, kp, preferred_element_type=jnp.float32) * SCALE
        s = s + ci.transpose(0, 2, 1)[..., None] - cum_k
        qpos = i * Q_BLOCK + jnp.arange(Q_BLOCK)
        mask = (kpos[None, :] <= qpos[:, None]) & kvalid[None, :]
        pr = jax.nn.softmax(jnp.where(mask, s, NEG_INF), axis=-1)
        return jnp.einsum('bhqk,bkhd->bqhd', pr.astype(vp.dtype), vp)

    o = lax.map(block, (jnp.arange(nb), qb, cb))
    return o.swapaxes(0, 1).reshape(B, Lt, N_HEADS, HEAD_DIM)[:, lead:lead + L]


def fox_sample(q, k, v, logf, k_past, v_past, lf_past):
    P, T = k_past.shape[1], q.shape[1]
    K = jnp.concatenate([k_past.astype(k.dtype), k], axis=1)
    V = jnp.concatenate([v_past.astype(v.dtype), v], axis=1)
    cum = jnp.cumsum(jnp.concatenate([lf_past.astype(jnp.float32), logf], axis=1), axis=1)
    s = jnp.einsum('bqhd,bkhd->bhqk', q, K, preferred_element_type=jnp.float32) * SCALE
    s = s + cum[:, P:].transpose(0, 2, 1)[..., None] - cum.transpose(0, 2, 1)[:, :, None, :]
    mask = jnp.arange(P + T)[None, :] <= (P + jnp.arange(T))[:, None]
    pr = jax.nn.softmax(jnp.where(mask, s, NEG_INF), axis=-1)
    return jnp.einsum('bhqk,bkhd->bqhd', pr.astype(V.dtype), V)


def _complex_affine_combine(e1, e2):
    a1r, a1i, b1r, b1i = e1
    a2r, a2i, b2r, b2i = e2
    return (a2r * a1r - a2i * a1i, a2r * a1i + a2i * a1r,
            a2r * b1r - a2i * b1i + b2r, a2r * b1i + a2i * b1r + b2i)


def s5_mix(u, x0_re, x0_im, lam_re, lam_im, log_dt, b_re, b_im, c_re, c_im, d, w_glu, b_glu):
    f32 = jnp.float32
    dt = jnp.exp(log_dt.astype(f32))[:, None]
    lr, li = lam_re.astype(f32), lam_im.astype(f32)
    mag = jnp.exp(lr * dt)
    ab_re, ab_im = mag * jnp.cos(li * dt), mag * jnp.sin(li * dt)
    nr, ni = ab_re - 1.0, ab_im
    den = lr * lr + li * li
    gr = (nr * lr + ni * li) / den
    gi = (ni * lr - nr * li) / den
    br, bi = b_re.astype(f32), b_im.astype(f32)
    bb_re = gr[..., None] * br - gi[..., None] * bi
    bb_im = gr[..., None] * bi + gi[..., None] * br
    uf = u.astype(f32)
    bu_re = jnp.einsum('btgc,gpc->btgp', uf, bb_re)
    bu_im = jnp.einsum('btgc,gpc->btgp', uf, bb_im)
    x0r, x0i = x0_re.astype(f32), x0_im.astype(f32)
    bu_re = bu_re.at[:, 0].add(ab_re * x0r - ab_im * x0i)
    bu_im = bu_im.at[:, 0].add(ab_re * x0i + ab_im * x0r)
    a_re = jnp.broadcast_to(ab_re, bu_re.shape)
    a_im = jnp.broadcast_to(ab_im, bu_im.shape)
    _, _, xr, xi = lax.associative_scan(_complex_affine_combine, (a_re, a_im, bu_re, bu_im), axis=1)
    y = (jnp.einsum('btgp,gcp->btgc', xr, c_re.astype(f32))
         - jnp.einsum('btgp,gcp->btgc', xi, c_im.astype(f32))
         + d.astype(f32).reshape(N_SSM_GROUPS, SSM_GROUP) * uf)
    y = jax.nn.gelu(y.reshape(u.shape[:2] + (D_SSM,)))
    out = y * jax.nn.sigmoid(y @ w_glu.astype(f32) + b_glu.astype(f32))
    return out.astype(u.dtype), xr[:, -1], xi[:, -1]


def pool_mix(p, buf, pos0, w_pool, s_pool):
    B, T = p.shape[0], p.shape[1]
    full = jnp.concatenate([buf.astype(p.dtype), p], axis=1)
    ff = full.astype(jnp.float32)
    cs = jnp.concatenate([jnp.zeros((B, 1, D_POOL), jnp.float32), jnp.cumsum(ff, axis=1)], axis=1)
    pos = pos0 + jnp.arange(T)
    outs = []
    for gi, w in enumerate(POOL_WINDOWS):
        sl = slice(gi * POOL_GROUP, (gi + 1) * POOL_GROUP)
        hi = cs[:, POOL_BUF + 1:POOL_BUF + 1 + T, sl]
        lo = cs[:, POOL_BUF + 1 - w:POOL_BUF + 1 - w + T, sl]
        cnt = jnp.minimum(w, pos + 1).astype(jnp.float32)[None, :, None]
        outs.append((hi - lo) / cnt - ff[:, POOL_BUF:, sl])
    m = jnp.stack(outs, axis=2)
    y = jnp.einsum('btgc,gcd->btgd', m, w_pool.astype(jnp.float32)).reshape(B, T, D_POOL)
    y = y * s_pool.astype(jnp.float32)
    return y.astype(p.dtype), full[:, -POOL_BUF:]


def moe(h, w_router, b_router, w_gate, w_up, w_down):
    f32 = jnp.float32
    n = h.shape[0]
    aff = jax.nn.sigmoid(h.astype(f32) @ w_router.astype(f32))
    biased = aff + b_router.astype(f32)
    gscore = lax.top_k(biased.reshape(n, N_EXPERT_GROUPS, EXPERTS_PER_GROUP), 2)[0].sum(-1)
    gsel = jnp.argmax(gscore, axis=-1)
    in_group = (jnp.arange(N_EXPERTS) // EXPERTS_PER_GROUP)[None, :] == gsel[:, None]
    _, idx = lax.top_k(jnp.where(in_group, biased, NEG_INF), TOP_K)
    w_sel = jnp.take_along_axis(aff, idx, axis=-1)
    w_sel = w_sel / w_sel.sum(-1, keepdims=True)
    gates = jnp.einsum('nk,nke->ne', w_sel, jax.nn.one_hot(idx, N_EXPERTS, dtype=f32))
    a = jnp.einsum('nd,edf->nef', h, w_gate)
    b = jnp.einsum('nd,edf->nef', h, w_up)
    hid = jax.nn.silu(a) * b * gates[..., None].astype(h.dtype)
    return jnp.einsum('nef,efd->nd', hid, w_down)


def post_norm_block(x, mix, w_out, ln1_g, ln1_b, w_router, b_router, w_gate, w_up, w_down, ln2_g, ln2_b):
    h = layer_norm(ALPHA * x + mix @ w_out, ln1_g, ln1_b)
    f = moe(h.reshape(-1, D_MODEL), w_router, b_router, w_gate, w_up, w_down).reshape(h.shape)
    return layer_norm(ALPHA * h + f, ln2_g, ln2_b)


def setup_inputs(seed: int = 0) -> dict:
    key = jax.random.key(seed)
    ks = iter(jax.random.split(key, 48))
    f32 = jnp.float32

    def nrm(shape, scale=1.0, mean=0.0):
        return mean + scale * jax.random.normal(next(ks), shape, f32)

    n_pages = PAST_LEN // PAGE_SIZE
    n_used = DEC_BATCH * n_pages
    n_pool = n_used + (n_used + 3) // 4
    page_table = jax.random.permutation(next(ks), n_pool)[:n_used].reshape(DEC_BATCH, n_pages).astype(jnp.int32)
    n_idx = jnp.arange(SSM_STATE, dtype=f32)
    return {
        'x_prompt': nrm((BATCH, SEQ, D_MODEL)),
        'x_sample': nrm((DEC_BATCH, DEC_SEQ, D_MODEL)),
        'cache_k': nrm((n_pool, DEPTH, PAGE_SIZE, N_HEADS, HEAD_DIM)),
        'cache_v': nrm((n_pool, DEPTH, PAGE_SIZE, N_HEADS, HEAD_DIM)),
        'cache_logf': jax.nn.log_sigmoid(nrm((n_pool, DEPTH, PAGE_SIZE, N_HEADS), 1.0, 2.0)),
        'page_table': page_table,
        'state_ssm_re': nrm((DEC_BATCH, DEPTH, N_SSM_GROUPS, SSM_STATE), 0.3),
        'state_ssm_im': nrm((DEC_BATCH, DEPTH, N_SSM_GROUPS, SSM_STATE), 0.3),
        'state_pool': nrm((DEC_BATCH, DEPTH, POOL_BUF, D_POOL)),
        'meta_tokens': nrm((N_META, D_MODEL)),
        'ln0_g': nrm((D_MODEL,), 0.02, 1.0),
        'ln0_b': nrm((D_MODEL,), 0.02),
        'w_in': nrm((DEPTH, D_MODEL, D_IN), D_MODEL ** -0.5),
        'b_forget': nrm((DEPTH, N_HEADS), 0.5, 2.0),
        'ssm_lam_re': nrm((DEPTH, N_SSM_GROUPS, SSM_STATE), 0.01, -0.5),
        'ssm_lam_im': math.pi * n_idx + nrm((DEPTH, N_SSM_GROUPS, SSM_STATE), 0.01),
        'ssm_log_dt': jax.random.uniform(next(ks), (DEPTH, N_SSM_GROUPS), f32, math.log(DT_MIN), math.log(DT_MAX)),
        'ssm_b_re': nrm((DEPTH, N_SSM_GROUPS, SSM_STATE, SSM_GROUP), (2 * SSM_GROUP) ** -0.5),
        'ssm_b_im': nrm((DEPTH, N_SSM_GROUPS, SSM_STATE, SSM_GROUP), (2 * SSM_GROUP) ** -0.5),
        'ssm_c_re': nrm((DEPTH, N_SSM_GROUPS, SSM_GROUP, SSM_STATE), SSM_STATE ** -0.5),
        'ssm_c_im': nrm((DEPTH, N_SSM_GROUPS, SSM_GROUP, SSM_STATE), SSM_STATE ** -0.5),
        'ssm_d': nrm((DEPTH, D_SSM)),
        'w_glu': nrm((DEPTH, D_SSM, D_SSM), D_SSM ** -0.5),
        'b_glu': nrm((DEPTH, D_SSM), 0.02),
        'w_pool': nrm((DEPTH, len(POOL_WINDOWS), POOL_GROUP, POOL_GROUP), POOL_GROUP ** -0.5),
        's_pool': nrm((DEPTH, D_POOL), 0.02, 1.0),
        'w_out': nrm((DEPTH, D_MODEL, D_MODEL), BETA * D_MODEL ** -0.5),
        'ln1_g': nrm((DEPTH, D_MODEL), 0.02, 1.0),
        'ln1_b': nrm((DEPTH, D_MODEL), 0.02),
        'w_router': nrm((D_MODEL, N_EXPERTS), D_MODEL ** -0.5),
        'b_router': nrm((N_EXPERTS,), 0.01),
        'w_gate': nrm((DEPTH, N_EXPERTS, D_MODEL, D_EXPERT), D_MODEL ** -0.5),
        'w_up': nrm((DEPTH, N_EXPERTS, D_MODEL, D_EXPERT), D_MODEL ** -0.5),
        'w_down': nrm((DEPTH, N_EXPERTS, D_EXPERT, D_MODEL), BETA * D_EXPERT ** -0.5),
        'ln2_g': nrm((DEPTH, D_MODEL), 0.02, 1.0),
        'ln2_b': nrm((DEPTH, D_MODEL), 0.02),
    }


def reference(x_prompt, x_sample, cache_k, cache_v, cache_logf, page_table,
              state_ssm_re, state_ssm_im, state_pool,
              meta_tokens, ln0_g, ln0_b, w_in, b_forget,
              ssm_lam_re, ssm_lam_im, ssm_log_dt, ssm_b_re, ssm_b_im, ssm_c_re, ssm_c_im, ssm_d,
              w_glu, b_glu, w_pool, s_pool, w_out, ln1_g, ln1_b,
              w_router, b_router, w_gate, w_up, w_down, ln2_g, ln2_b):
    f32 = jnp.float32
    bp, bs = x_prompt.shape[0], x_sample.shape[0]
    n_prompt = N_META + x_prompt.shape[1]
    n_new = x_sample.shape[1]
    past_len = page_table.shape[1] * cache_k.shape[2]
    meta = jnp.broadcast_to(meta_tokens.astype(x_prompt.dtype)[None], (bp, N_META, D_MODEL))
    xp = layer_norm(jnp.concatenate([meta, x_prompt], axis=1), ln0_g, ln0_b)
    xs = layer_norm(x_sample, ln0_g, ln0_b)
    zero_state = jnp.zeros((bp, N_SSM_GROUPS, SSM_STATE), f32)
    zero_buf = jnp.zeros((bp, POOL_BUF, D_POOL), xp.dtype)
    kp_l, vp_l, lfp_l, srp_l, sip_l, pbp_l = [], [], [], [], [], []
    kq_l, vq_l, lfq_l, srq_l, siq_l, pbq_l = [], [], [], [], [], []
    for l in range(DEPTH):
        ssm_w = (ssm_lam_re[l], ssm_lam_im[l], ssm_log_dt[l], ssm_b_re[l], ssm_b_im[l],
                 ssm_c_re[l], ssm_c_im[l], ssm_d[l], w_glu[l], b_glu[l])
        blk_w = (w_out[l], ln1_g[l], ln1_b[l], w_router, b_router,
                 w_gate[l], w_up[l], w_down[l], ln2_g[l], ln2_b[l])
        q, k, v, lf, u, p = project(xp, w_in[l], b_forget[l])
        att = fox_prompt(q, k, v, lf).reshape(bp, n_prompt, D_ATT)
        ys, sr, si = s5_mix(u, zero_state, zero_state, *ssm_w)
        yq, pb = pool_mix(p, zero_buf, 0, w_pool[l], s_pool[l])
        xp = post_norm_block(xp, jnp.concatenate([att, ys, yq], axis=-1), *blk_w)
        kp_l.append(k); vp_l.append(v); lfp_l.append(lf)
        srp_l.append(sr); sip_l.append(si); pbp_l.append(pb)
        q, k, v, lf, u, p = project(xs, w_in[l], b_forget[l])
        k_past = cache_k[page_table, l].reshape(bs, past_len, N_HEADS, HEAD_DIM)
        v_past = cache_v[page_table, l].reshape(bs, past_len, N_HEADS, HEAD_DIM)
        lf_past = cache_logf[page_table, l].reshape(bs, past_len, N_HEADS)
        att = fox_sample(q, k, v, lf, k_past, v_past, lf_past).reshape(bs, n_new, D_ATT)
        ys, sr, si = s5_mix(u, state_ssm_re[:, l], state_ssm_im[:, l], *ssm_w)
        yq, pb = pool_mix(p, state_pool[:, l], past_len, w_pool[l], s_pool[l])
        xs = post_norm_block(xs, jnp.concatenate([att, ys, yq], axis=-1), *blk_w)
        kq_l.append(k); vq_l.append(v); lfq_l.append(lf)
        srq_l.append(sr); siq_l.append(si); pbq_l.append(pb)
    y_prompt = xp[:, N_META:]
    y_sample = xs
    return (y_prompt, y_sample,
            jnp.stack(kp_l, axis=1), jnp.stack(vp_l, axis=1), jnp.stack(lfp_l, axis=1),
            jnp.stack(srp_l, axis=1), jnp.stack(sip_l, axis=1), jnp.stack(pbp_l, axis=1),
            jnp.stack(kq_l, axis=1), jnp.stack(vq_l, axis=1), jnp.stack(lfq_l, axis=1),
            jnp.stack(srq_l, axis=1), jnp.stack(siq_l, axis=1), jnp.stack(pbq_l, axis=1))
```

```python
import functools
import math

import jax
import jax.numpy as jnp
from jax import lax
from jax.experimental import pallas as pl
from jax.experimental.pallas import tpu as pltpu

F32 = jnp.float32
BF16 = jnp.bfloat16

N_META = 16
N_HEADS = 8
HEAD_DIM = 64
D_ATT = N_HEADS * HEAD_DIM
SSM_GROUP = 16
N_SSM_GROUPS = 16
SSM_STATE = 64
D_SSM = SSM_GROUP * N_SSM_GROUPS
D_STATE = N_SSM_GROUPS * SSM_STATE
POOL_WINDOWS = (2, 4, 8, 16)
POOL_GROUP = 64
D_POOL = POOL_GROUP * len(POOL_WINDOWS)
POOL_BUF = max(POOL_WINDOWS) - 1
POOL_HIST = POOL_BUF + 1
N_EXPERTS = 16
N_EXPERT_GROUPS = 4
EXPERTS_PER_GROUP = N_EXPERTS // N_EXPERT_GROUPS
TOP_K = 2
SCALE = HEAD_DIM ** -0.5
LN_EPS = 1e-5
NEG_INF = -1e30

LANES = 128
SUBLANES = 8
VMEM_LIMIT = 56 * 1024 * 1024


def _cparams(n_axes):
    return pltpu.CompilerParams(
        dimension_semantics=("arbitrary",) * n_axes, vmem_limit_bytes=VMEM_LIMIT)


def _round_up(x, m):
    return (x + m - 1) // m * m


def _pick_tile(n, candidates):
    for c in candidates:
        if n % c == 0:
            return c
    raise ValueError(f"no tile for {n}")


def _layer_norm(x, g, b):
    mu = jnp.mean(x, axis=-1, keepdims=True)
    xc = x - mu
    var = jnp.mean(xc * xc, axis=-1, keepdims=True)
    return xc * lax.rsqrt(var + LN_EPS) * g + b


def _log_sigmoid(x):
    return -(jnp.maximum(-x, 0.0) + jnp.log1p(jnp.exp(-jnp.abs(x))))


def _dot(a, b):
    return jnp.dot(a, b, preferred_element_type=F32)


def _dot_nt(a, b, precision=None):
    return lax.dot_general(a, b, (((1,), (1,)), ((), ())),
                           preferred_element_type=F32, precision=precision)


def _lane_cumsum(x):
    lane = lax.broadcasted_iota(jnp.int32, x.shape, 1)
    s = 1
    while s < LANES:
        x = x + jnp.where(lane >= s, pltpu.roll(x, s, axis=1), 0.0)
        s *= 2
    return x


def _s5_param_kernel(lr_ref, li_ref, ldt_ref, br_ref, bi_ref, apow_re, apow_im, bbr_ref, bbi_ref):
    lr = lr_ref[...]
    li = li_ref[...]
    dt = jnp.exp(ldt_ref[...])
    mag = jnp.exp(lr * dt)
    ab_re = mag * jnp.cos(li * dt)
    ab_im = mag * jnp.sin(li * dt)
    nr, ni = ab_re - 1.0, ab_im
    den = lr * lr + li * li
    gr = (nr * lr + ni * li) / den
    gi = (ni * lr - nr * li) / den
    br = br_ref[...]
    bi = bi_ref[...]
    bbr_ref[...] = gr[:, None, :] * br - gi[:, None, :] * bi
    bbi_ref[...] = gr[:, None, :] * bi + gi[:, None, :] * br
    pr, pi = ab_re, ab_im
    apow_re[0] = pr
    apow_im[0] = pi
    for k in range(1, SUBLANES):
        pr, pi = pr * ab_re - pi * ab_im, pr * ab_im + pi * ab_re
        apow_re[k] = pr
        apow_im[k] = pi


def _s5_params(lam_re, lam_im, log_dt, b_re, b_im):
    g, p = lam_re.shape
    c = b_re.shape[-1]
    out_shape = (jax.ShapeDtypeStruct((SUBLANES, g, p), F32), jax.ShapeDtypeStruct((SUBLANES, g, p), F32),
                 jax.ShapeDtypeStruct((g, c, p), F32), jax.ShapeDtypeStruct((g, c, p), F32))
    return pl.pallas_call(_s5_param_kernel, out_shape=out_shape, name="s5_params")(
        lam_re, lam_im, log_dt.reshape(g, 1), b_re.transpose(0, 2, 1), b_im.transpose(0, 2, 1))


def _block_diag(w):
    g, a, b = w.shape
    eye = jnp.eye(g, dtype=w.dtype)
    return (w[:, :, None, :] * eye[:, None, :, None]).reshape(g * a, g * b)


def _inproj_prompt_kernel(apply_ln, tm, x_ref, g_ref, b_ref, wq_ref, wup_ref, wt_ref, bf_ref, *outs):
    if apply_ln:
        xn_ref, q_ref, kt_ref, vt_ref, lft_ref, ckt_ref, u_ref, p_ref, carry_ref = outs
    else:
        q_ref, kt_ref, vt_ref, lft_ref, ckt_ref, u_ref, p_ref, carry_ref = outs
    t = pl.program_id(1)
    x = x_ref[0]
    if apply_ln:
        x = _layer_norm(x, g_ref[...], b_ref[...])
        xn_ref[0] = x
    xb = x.astype(BF16)
    q = _dot(xb, wq_ref[...]) * SCALE
    for h in range(N_HEADS):
        q_ref[0, h] = q[:, h * HEAD_DIM:(h + 1) * HEAD_DIM].astype(BF16)
    up = _dot(xb, wup_ref[...])
    u_ref[0] = up[:, :D_SSM]
    p_ref[0] = up[:, D_SSM:]
    zt = _dot_nt(wt_ref[...], xb)
    kt_ref[0] = zt[:D_ATT]
    vt_ref[0] = zt[D_ATT:2 * D_ATT]
    lf = _log_sigmoid(zt[2 * D_ATT:] + bf_ref[...])
    lft_ref[0] = lf

    @pl.when(t == 0)
    def _():
        carry_ref[...] = jnp.zeros_like(carry_ref)

    carry = carry_ref[...]
    for c in range(tm // LANES):
        blk = _lane_cumsum(lf[:, c * LANES:(c + 1) * LANES]) + carry
        ckt_ref[0, :, c * LANES:(c + 1) * LANES] = blk
        carry = jnp.broadcast_to(blk[:, LANES - 1:LANES], carry.shape)
    carry_ref[...] = carry


def _inproj_prompt(x, ln_g, ln_b, wq, wup, wt, bf, apply_ln, tm):
    b, lp, d = x.shape
    nt = lp // tm
    full = lambda shape: pl.BlockSpec(shape, lambda i, j: (0,) * len(shape))
    out_shape, out_specs = [], []
    if apply_ln:
        out_shape.append(jax.ShapeDtypeStruct((b, lp, d), F32))
        out_specs.append(pl.BlockSpec((1, tm, d), lambda i, j: (i, j, 0)))
    out_shape += [
        jax.ShapeDtypeStruct((b, N_HEADS, lp, HEAD_DIM), BF16),
        jax.ShapeDtypeStruct((b, D_ATT, lp), F32),
        jax.ShapeDtypeStruct((b, D_ATT, lp), F32),
        jax.ShapeDtypeStruct((b, N_HEADS, lp), F32),
        jax.ShapeDtypeStruct((b, N_HEADS, lp), F32),
        jax.ShapeDtypeStruct((b, lp, D_SSM), F32),
        jax.ShapeDtypeStruct((b, lp, D_POOL), F32),
    ]
    out_specs += [
        pl.BlockSpec((1, N_HEADS, tm, HEAD_DIM), lambda i, j: (i, 0, j, 0)),
        pl.BlockSpec((1, D_ATT, tm), lambda i, j: (i, 0, j)),
        pl.BlockSpec((1, D_ATT, tm), lambda i, j: (i, 0, j)),
        pl.BlockSpec((1, N_HEADS, tm), lambda i, j: (i, 0, j)),
        pl.BlockSpec((1, N_HEADS, tm), lambda i, j: (i, 0, j)),
        pl.BlockSpec((1, tm, D_SSM), lambda i, j: (i, j, 0)),
        pl.BlockSpec((1, tm, D_POOL), lambda i, j: (i, j, 0)),
    ]
    return pl.pallas_call(
        functools.partial(_inproj_prompt_kernel, apply_ln, tm),
        grid=(b, nt),
        in_specs=[pl.BlockSpec((1, tm, d), lambda i, j: (i, j, 0)),
                  full(ln_g.shape), full(ln_b.shape), full(wq.shape), full(wup.shape),
                  full(wt.shape), full(bf.shape)],
        out_specs=out_specs,
        out_shape=out_shape,
        scratch_shapes=[pltpu.VMEM((N_HEADS, LANES), F32)],
        compiler_params=_cparams(2),
        name="inproj_prompt",
    )(x, ln_g, ln_b, wq, wup, wt, bf)


def _inproj_sample_kernel(apply_ln, x_ref, g_ref, b_ref, w_ref, wf_ref, bf_ref, *outs):
    if apply_ln:
        xn_ref, qkv_ref, up_ref, lft_ref = outs
    else:
        qkv_ref, up_ref, lft_ref = outs
    x = x_ref[...]
    if apply_ln:
        x = _layer_norm(x, g_ref[...], b_ref[...])
        xn_ref[...] = x
    xb = x.astype(BF16)
    z = _dot(xb, w_ref[...])
    qkv_ref[...] = z[:, :3 * D_ATT]
    up_ref[...] = z[:, 3 * D_ATT:]
    lft_ref[...] = _log_sigmoid(_dot_nt(wf_ref[...], xb) + bf_ref[...])


def _inproj_sample(x, ln_g, ln_b, w, wf, bf, apply_ln, tm):
    n, d = x.shape
    full = lambda shape: pl.BlockSpec(shape, lambda i: (0,) * len(shape))
    out_shape, out_specs = [], []
    if apply_ln:
        out_shape.append(jax.ShapeDtypeStruct((n, d), F32))
        out_specs.append(pl.BlockSpec((tm, d), lambda i: (i, 0)))
    out_shape += [jax.ShapeDtypeStruct((n, 3 * D_ATT), F32),
                  jax.ShapeDtypeStruct((n, D_SSM + D_POOL), F32),
                  jax.ShapeDtypeStruct((N_HEADS, n), F32)]
    out_specs += [pl.BlockSpec((tm, 3 * D_ATT), lambda i: (i, 0)),
                  pl.BlockSpec((tm, D_SSM + D_POOL), lambda i: (i, 0)),
                  pl.BlockSpec((N_HEADS, tm), lambda i: (0, i))]
    return pl.pallas_call(
        functools.partial(_inproj_sample_kernel, apply_ln),
        grid=(n // tm,),
        in_specs=[pl.BlockSpec((tm, d), lambda i: (i, 0)),
                  full(ln_g.shape), full(ln_b.shape), full(w.shape), full(wf.shape), full(bf.shape)],
        out_specs=out_specs,
        out_shape=out_shape,
        compiler_params=_cparams(1),
        name="inproj_sample",
    )(x, ln_g, ln_b, w, wf, bf)


HEADS_PER_STEP = 2


def _attn_prompt_kernel(tq, q_ref, kt_ref, vt_ref, ck_ref, o_ref, kb_ref, vb_ref):
    qi = pl.program_id(2)

    @pl.when(qi == 0)
    def _():
        kb_ref[...] = kt_ref[0].astype(BF16)
        vb_ref[...] = vt_ref[0].astype(BF16)

    row = lax.broadcasted_iota(jnp.int32, (tq, tq), 0)
    col = lax.broadcasted_iota(jnp.int32, (tq, tq), 1)
    causal = col <= row
    outs = []
    for j in range(HEADS_PER_STEP):
        q = q_ref[0, j]
        hs = slice(j * HEAD_DIM, (j + 1) * HEAD_DIM)

        def block(ki, carry, masked):
            m, l, acc = carry
            k0 = pl.multiple_of(ki * tq, LANES)
            s = _dot(q, kb_ref[hs, pl.ds(k0, tq)])
            s = s - ck_ref[0, 0, j:j + 1, pl.ds(k0, tq)]
            if masked:
                s = jnp.where(causal, s, NEG_INF)
            m_new = jnp.maximum(m, jnp.max(s, axis=1, keepdims=True))
            p = jnp.exp(s - m_new)
            alpha = jnp.exp(m - m_new)
            l = alpha * l + jnp.sum(p, axis=1, keepdims=True)
            acc = alpha * acc + _dot_nt(p.astype(BF16), vb_ref[hs, pl.ds(k0, tq)])
            return m_new, l, acc

        init = (jnp.full((tq, 1), NEG_INF, F32), jnp.zeros((tq, 1), F32),
                jnp.zeros((tq, HEAD_DIM), F32))
        carry = lax.fori_loop(0, qi, lambda ki, c: block(ki, c, False), init)
        m, l, acc = block(qi, carry, True)
        outs.append(acc / l)
    o_ref[0] = jnp.concatenate(outs, axis=1).astype(o_ref.dtype)


def _attn_prompt(q, kt, vt, ck, tq):
    b, _, lp, _ = q.shape
    hp = N_HEADS // HEADS_PER_STEP
    rows = HEADS_PER_STEP * HEAD_DIM
    return pl.pallas_call(
        functools.partial(_attn_prompt_kernel, tq),
        grid=(b, hp, lp // tq),
        in_specs=[pl.BlockSpec((1, HEADS_PER_STEP, tq, HEAD_DIM), lambda i, h, j: (i, h, j, 0)),
                  pl.BlockSpec((1, rows, lp), lambda i, h, j: (i, h, 0)),
                  pl.BlockSpec((1, rows, lp), lambda i, h, j: (i, h, 0)),
                  pl.BlockSpec((1, 1, HEADS_PER_STEP, lp), lambda i, h, j: (i, h, 0, 0))],
        out_specs=pl.BlockSpec((1, tq, rows), lambda i, h, j: (i, j, h)),
        out_shape=jax.ShapeDtypeStruct((b, lp, D_ATT), BF16),
        scratch_shapes=[pltpu.VMEM((rows, lp), BF16), pltpu.VMEM((rows, lp), BF16)],
        compiler_params=_cparams(3),
        name="attn_prompt",
    )(q, kt, vt, ck.reshape(b, hp, HEADS_PER_STEP, lp))


def _attn_decode_kernel(n_pages, t_new, pt_ref, qkv_ref, lfn_ref, *refs):
    k_refs = refs[:n_pages]
    v_refs = refs[n_pages:2 * n_pages]
    lf_refs = refs[2 * n_pages:3 * n_pages]
    o_ref = refs[3 * n_pages]
    rows = t_new * N_HEADS
    qkv = qkv_ref[0]
    q = qkv[:, :D_ATT] * SCALE
    k_new = qkv[:, D_ATT:2 * D_ATT]
    v_new = qkv[:, 2 * D_ATT:]
    head_of_row = lax.broadcasted_iota(jnp.int32, (N_HEADS, D_ATT), 0)
    head_of_col = lax.broadcasted_iota(jnp.int32, (N_HEADS, D_ATT), 1) // HEAD_DIM
    hmask = head_of_row == head_of_col
    qbd = jnp.concatenate(
        [jnp.where(hmask, jnp.broadcast_to(q[t:t + 1], (N_HEADS, D_ATT)), 0.0) for t in range(t_new)],
        axis=0).astype(BF16)

    def expand(c):
        return jnp.concatenate([c] * t_new, axis=0)

    s_blocks = []
    off = jnp.zeros((N_HEADS, LANES), F32)
    for j in range(n_pages):
        ck = _lane_cumsum(lf_refs[j][0, 0]) + off
        off = jnp.broadcast_to(ck[:, LANES - 1:LANES], off.shape)
        s = _dot(qbd, k_refs[j][0, 0].astype(BF16))
        s_blocks.append(s - expand(ck))
    pad = jnp.zeros((LANES - t_new, D_ATT), F32)
    k_pad = jnp.concatenate([k_new, pad], axis=0).astype(BF16)
    v_pad = jnp.concatenate([v_new, pad], axis=0).astype(BF16)
    ck_new = _lane_cumsum(lfn_ref[0]) + off
    s_new = _dot_nt(qbd, k_pad) - expand(ck_new)
    key_t = lax.broadcasted_iota(jnp.int32, (rows, LANES), 1)
    qry_t = lax.broadcasted_iota(jnp.int32, (rows, LANES), 0) // N_HEADS
    s_blocks.append(jnp.where(key_t <= qry_t, s_new, NEG_INF))

    m = s_blocks[0].max(axis=1, keepdims=True)
    for s in s_blocks[1:]:
        m = jnp.maximum(m, s.max(axis=1, keepdims=True))
    l = jnp.zeros((rows, 1), F32)
    acc = jnp.zeros((rows, D_ATT), F32)
    for j, s in enumerate(s_blocks):
        p = jnp.exp(s - m)
        l = l + p.sum(axis=1, keepdims=True)
        if j < n_pages:
            acc = acc + _dot_nt(p.astype(BF16), v_refs[j][0, 0].astype(BF16))
        else:
            acc = acc + _dot(p.astype(BF16), v_pad)
    acc = acc / l
    omask = jnp.concatenate([hmask] * t_new, axis=0)
    acc = jnp.where(omask, acc, 0.0)
    o_ref[0] = acc.reshape(t_new, N_HEADS, D_ATT).sum(axis=1).astype(o_ref.dtype)


def _attn_decode(page_table, qkv, lf_new_t, kt_pages, vt_pages, lf_pages, layer):
    bs, t_new, _ = qkv.shape
    n_pages = page_table.shape[1]
    page = kt_pages.shape[-1]
    pt_flat = page_table.reshape(-1)

    def page_spec(rows, j):
        return pl.BlockSpec((1, 1, rows, page), lambda i, pt: (pt[i * n_pages + j], layer, 0, 0))

    in_specs = [pl.BlockSpec((1, t_new, 3 * D_ATT), lambda i, pt: (i, 0, 0)),
                pl.BlockSpec((1, N_HEADS, LANES), lambda i, pt: (i, 0, 0))]
    in_specs += [page_spec(D_ATT, j) for j in range(n_pages)]
    in_specs += [page_spec(D_ATT, j) for j in range(n_pages)]
    in_specs += [page_spec(N_HEADS, j) for j in range(n_pages)]
    grid_spec = pltpu.PrefetchScalarGridSpec(
        num_scalar_prefetch=1, grid=(bs,), in_specs=in_specs,
        out_specs=pl.BlockSpec((1, t_new, D_ATT), lambda i, pt: (i, 0, 0)))
    return pl.pallas_call(
        functools.partial(_attn_decode_kernel, n_pages, t_new),
        grid_spec=grid_spec,
        out_shape=jax.ShapeDtypeStruct((bs, t_new, D_ATT), BF16),
        compiler_params=_cparams(1),
        name="attn_decode",
    )(pt_flat, qkv, lf_new_t, *([kt_pages] * n_pages), *([vt_pages] * n_pages), *([lf_pages] * n_pages))


def _seqmix_kernel(tt, snap_tile, snap_row, pos0,
                   u_ref, p_ref, x0r_ref, x0i_ref, hist_ref,
                   bbr_ref, bbi_ref, cr_ref, ci_ref, d_ref, ak_ref, wglu_ref, bglu_ref,
                   wpool_ref, spool_ref,
                   ys_ref, yq_ref, sr_ref, si_ref,
                   xr_scr, xi_scr, car_r, car_i, ext_scr):
    t = pl.program_id(1)

    @pl.when(t == 0)
    def _():
        car_r[...] = x0r_ref[0]
        car_i[...] = x0i_ref[0]
        ext_scr[0:POOL_HIST, :] = hist_ref[0]

    u = u_ref[0]
    ub = u.astype(BF16)
    xr_scr[...] = _dot(ub, bbr_ref[...])
    xi_scr[...] = _dot(ub, bbi_ref[...])

    def scan_block(i, carry):
        cr, ci = carry
        r0 = pl.multiple_of(i * SUBLANES, SUBLANES)
        br = xr_scr[pl.ds(r0, SUBLANES), :]
        bi = xi_scr[pl.ds(r0, SUBLANES), :]
        for idx, k in enumerate((1, 2, 4)):
            akr = ak_ref[2 * idx]
            aki = ak_ref[2 * idx + 1]
            sr = pltpu.roll(br, k, axis=0)
            si = pltpu.roll(bi, k, axis=0)
            br, bi = br + akr * sr - aki * si, bi + akr * si + aki * sr
        pr = ak_ref[6]
        pi = ak_ref[7]
        xr = br + pr * cr - pi * ci
        xi = bi + pr * ci + pi * cr
        xr_scr[pl.ds(r0, SUBLANES), :] = xr
        xi_scr[pl.ds(r0, SUBLANES), :] = xi
        return xr[SUBLANES - 1:SUBLANES], xi[SUBLANES - 1:SUBLANES]

    cr, ci = lax.fori_loop(0, tt // SUBLANES, scan_block, (car_r[...], car_i[...]))
    car_r[...] = cr
    car_i[...] = ci

    @pl.when(t == snap_tile)
    def _():
        sr_ref[0] = xr_scr[snap_row:snap_row + 1, :]
        si_ref[0] = xi_scr[snap_row:snap_row + 1, :]

    y = (_dot(xr_scr[...].astype(BF16), cr_ref[...]) - _dot(xi_scr[...].astype(BF16), ci_ref[...])
         + d_ref[...] * u)
    y = jax.nn.gelu(y)
    gate = jax.nn.sigmoid(_dot(y.astype(BF16), wglu_ref[...]) + bglu_ref[...])
    ys_ref[0] = (y * gate).astype(ys_ref.dtype)

    ext_scr[POOL_HIST:POOL_HIST + tt, :] = p_ref[0]
    e = ext_scr[...]
    c2 = e + pltpu.roll(e, 1, axis=0)
    c4 = c2 + pltpu.roll(c2, 2, axis=0)
    c8 = c4 + pltpu.roll(c4, 4, axis=0)
    c16 = c8 + pltpu.roll(c8, 8, axis=0)
    lane = lax.broadcasted_iota(jnp.int32, (tt, D_POOL), 1)
    grp = lane // POOL_GROUP
    win = jnp.where(grp == 0, c2[POOL_HIST:],
                    jnp.where(grp == 1, c4[POOL_HIST:], jnp.where(grp == 2, c8[POOL_HIST:], c16[POOL_HIST:])))
    wsize = jnp.where(grp == 0, 2, jnp.where(grp == 1, 4, jnp.where(grp == 2, 8, 16)))
    pos = pos0 + t * tt + lax.broadcasted_iota(jnp.int32, (tt, D_POOL), 0)
    cnt = jnp.minimum(wsize, pos + 1).astype(F32)
    mm = win / cnt - e[POOL_HIST:]
    yq = _dot(mm.astype(BF16), wpool_ref[...]) * spool_ref[...]
    yq_ref[0] = yq.astype(yq_ref.dtype)
    ext_scr[0:POOL_HIST, :] = e[tt:tt + POOL_HIST]


def _seqmix(u, p, x0r, x0i, hist, prm, length, pos0, tt):
    b, lp, _ = u.shape
    nt = lp // tt
    snap_tile, snap_row = (length - 1) // tt, (length - 1) % tt
    full = lambda a: pl.BlockSpec(a.shape, lambda i, j: (0,) * a.ndim)
    weights = (prm["bbr"], prm["bbi"], prm["cr"], prm["ci"], prm["d"], prm["ak"], prm["wglu"],
               prm["bglu"], prm["wpool"], prm["spool"])
    return pl.pallas_call(
        functools.partial(_seqmix_kernel, tt, snap_tile, snap_row, pos0),
        grid=(b, nt),
        in_specs=[pl.BlockSpec((1, tt, D_SSM), lambda i, j: (i, j, 0)),
                  pl.BlockSpec((1, tt, D_POOL), lambda i, j: (i, j, 0)),
                  pl.BlockSpec((1, 1, D_STATE), lambda i, j: (i, 0, 0)),
                  pl.BlockSpec((1, 1, D_STATE), lambda i, j: (i, 0, 0)),
                  pl.BlockSpec((1, POOL_HIST, D_POOL), lambda i, j: (i, 0, 0))]
                 + [full(w) for w in weights],
        out_specs=[pl.BlockSpec((1, tt, D_SSM), lambda i, j: (i, j, 0)),
                   pl.BlockSpec((1, tt, D_POOL), lambda i, j: (i, j, 0)),
                   pl.BlockSpec((1, 1, D_STATE), lambda i, j: (i, 0, 0)),
                   pl.BlockSpec((1, 1, D_STATE), lambda i, j: (i, 0, 0))],
        out_shape=[jax.ShapeDtypeStruct((b, lp, D_SSM), BF16),
                   jax.ShapeDtypeStruct((b, lp, D_POOL), BF16),
                   jax.ShapeDtypeStruct((b, 1, D_STATE), F32),
                   jax.ShapeDtypeStruct((b, 1, D_STATE), F32)],
        scratch_shapes=[pltpu.VMEM((tt, D_STATE), F32), pltpu.VMEM((tt, D_STATE), F32),
                        pltpu.VMEM((1, D_STATE), F32), pltpu.VMEM((1, D_STATE), F32),
                        pltpu.VMEM((POOL_HIST + tt, D_POOL), F32)],
        compiler_params=_cparams(2),
        name="seqmix",
    )(u, p, x0r, x0i, hist, *weights)


def _outproj_kernel(alpha, x_ref, att_ref, ys_ref, yq_ref, wo_ref, g_ref, b_ref, wr_ref, br_ref,
                    h_ref, gates_ref):
    y = alpha * x_ref[...]
    y = y + _dot(att_ref[...], wo_ref[0:D_ATT, :])
    y = y + _dot(ys_ref[...], wo_ref[D_ATT:D_ATT + D_SSM, :])
    y = y + _dot(yq_ref[...], wo_ref[D_ATT + D_SSM:, :])
    h = _layer_norm(y, g_ref[...], b_ref[...])
    h_ref[...] = h
    logits = _dot_nt(wr_ref[...], h, precision=lax.Precision.HIGHEST)
    aff = jax.nn.sigmoid(logits)
    biased = aff + br_ref[...]
    rows = [biased[e:e + 1, :] for e in range(N_EXPERTS)]

    def beats(a, ia, c, ic):
        return (a >= c) if ia < ic else (a > c)

    gscore = []
    for g in range(N_EXPERT_GROUPS):
        members = list(range(g * EXPERTS_PER_GROUP, (g + 1) * EXPERTS_PER_GROUP))
        total = jnp.zeros_like(rows[0])
        for e in members:
            rank = jnp.zeros_like(rows[0])
            for o in members:
                if o != e:
                    rank = rank + beats(rows[o], o, rows[e], e).astype(F32)
            total = total + jnp.where(rank < 2.0, rows[e], 0.0)
        gscore.append(total)
    in_group = []
    for g in range(N_EXPERT_GROUPS):
        lost = jnp.zeros_like(rows[0])
        for o in range(N_EXPERT_GROUPS):
            if o != g:
                lost = lost + beats(gscore[o], o, gscore[g], g).astype(F32)
        in_group.append(lost < 1.0)
    masked = [jnp.where(in_group[e // EXPERTS_PER_GROUP], rows[e], NEG_INF) for e in range(N_EXPERTS)]
    sel = []
    for e in range(N_EXPERTS):
        rank = jnp.zeros_like(rows[0])
        for o in range(N_EXPERTS):
            if o != e:
                rank = rank + beats(masked[o], o, masked[e], e).astype(F32)
        sel.append(jnp.where(rank < float(TOP_K), aff[e:e + 1, :], 0.0))
    denom = sel[0]
    for e in range(1, N_EXPERTS):
        denom = denom + sel[e]
    gates_ref[...] = jnp.concatenate(sel, axis=0) / denom


def _outproj(x, att, ys, yq, wo, ln_g, ln_b, wr_t, br, alpha, tm):
    n, d = x.shape
    full = lambda a: pl.BlockSpec(a.shape, lambda i: (0,) * a.ndim)
    row = lambda w: pl.BlockSpec((tm, w), lambda i: (i, 0))
    return pl.pallas_call(
        functools.partial(_outproj_kernel, alpha),
        grid=(n // tm,),
        in_specs=[row(d), row(D_ATT), row(D_SSM), row(D_POOL), full(wo), full(ln_g), full(ln_b),
                  full(wr_t), full(br)],
        out_specs=[row(d), pl.BlockSpec((N_EXPERTS, tm), lambda i: (0, i))],
        out_shape=[jax.ShapeDtypeStruct((n, d), F32), jax.ShapeDtypeStruct((N_EXPERTS, n), F32)],
        compiler_params=_cparams(1),
        name="outproj_ln_router",
    )(x, att, ys, yq, wo, ln_g, ln_b, wr_t, br)


def _moe_kernel(alpha, d_expert, h_ref, gates_ref, wg_ref, wu_ref, wd_ref, g_ref, b_ref, o_ref,
                hb_ref, acc_ref):
    g = pl.program_id(1)

    @pl.when(g == 0)
    def _():
        hb_ref[...] = h_ref[...].astype(BF16)
        acc_ref[...] = jnp.zeros_like(acc_ref)

    hb = hb_ref[...]
    a = _dot(hb, wg_ref[...])
    b = _dot(hb, wu_ref[...])
    gates = gates_ref[0]
    parts = []
    for e in range(EXPERTS_PER_GROUP):
        cs = slice(e * d_expert, (e + 1) * d_expert)
        hid = jax.nn.silu(a[:, cs]) * b[:, cs] * gates[:, e:e + 1]
        parts.append(hid.astype(BF16))
    acc_ref[...] += _dot(jnp.concatenate(parts, axis=1), wd_ref[...])

    @pl.when(g == N_EXPERT_GROUPS - 1)
    def _():
        y = alpha * h_ref[...] + acc_ref[...]
        o_ref[...] = _layer_norm(y, g_ref[...], b_ref[...])


def _moe(h, gates_g, wg, wu, wd, ln_g, ln_b, alpha, tm):
    n, d = h.shape
    gw = wg.shape[1] // N_EXPERT_GROUPS
    d_expert = gw // EXPERTS_PER_GROUP
    full = lambda a: pl.BlockSpec(a.shape, lambda i, j: (0,) * a.ndim)
    return pl.pallas_call(
        functools.partial(_moe_kernel, alpha, d_expert),
        grid=(n // tm, N_EXPERT_GROUPS),
        in_specs=[pl.BlockSpec((tm, d), lambda i, j: (i, 0)),
                  pl.BlockSpec((1, tm, EXPERTS_PER_GROUP), lambda i, j: (j, i, 0)),
                  pl.BlockSpec((d, gw), lambda i, j: (0, j)),
                  pl.BlockSpec((d, gw), lambda i, j: (0, j)),
                  pl.BlockSpec((gw, d), lambda i, j: (j, 0)),
                  full(ln_g), full(ln_b)],
        out_specs=pl.BlockSpec((tm, d), lambda i, j: (i, 0)),
        out_shape=jax.ShapeDtypeStruct((n, d), F32),
        scratch_shapes=[pltpu.VMEM((tm, d), BF16), pltpu.VMEM((tm, d), F32)],
        compiler_params=_cparams(2),
        name="moe_ln",
    )(h, gates_g, wg, wu, wd, ln_g, ln_b)


def _post_block(x, att, ys, yq, lw, alpha, tm):
    n = x.shape[0]
    h, gates_t = _outproj(x, att, ys, yq, lw["wo"], lw["ln1_g"], lw["ln1_b"], lw["wr_t"], lw["br"], alpha, tm)
    gates_g = gates_t.reshape(N_EXPERT_GROUPS, EXPERTS_PER_GROUP, n).transpose(0, 2, 1)
    return _moe(h, gates_g, lw["wg"], lw["wu"], lw["wd"], lw["ln2_g"], lw["ln2_b"], alpha, tm)


def kernel(x_prompt, x_sample, cache_k, cache_v, cache_logf, page_table, state_ssm_re, state_ssm_im, state_pool, meta_tokens, ln0_g, ln0_b, w_in, b_forget, ssm_lam_re, ssm_lam_im, ssm_log_dt, ssm_b_re, ssm_b_im, ssm_c_re, ssm_c_im, ssm_d, w_glu, b_glu, w_pool, s_pool, w_out, ln1_g, ln1_b, w_router, b_router, w_gate, w_up, w_down, ln2_g, ln2_b):
    bp, seq, d_model = x_prompt.shape
    bs, t_new, _ = x_sample.shape
    depth = w_in.shape[0]
    n_pages, page = page_table.shape[1], cache_k.shape[2]
    past_len = n_pages * page
    length = N_META + seq
    lp = _round_up(length, LANES)
    tile = _pick_tile(lp, (640, 512, 384, 256, 128))
    alpha = (2 * depth) ** 0.25
    n_s = bs * t_new
    tm_s = _pick_tile(n_s, (512, 256, 128, 64, 32, 16, 8))
    assert t_new == SUBLANES and page == LANES and past_len >= POOL_BUF

    o1, o2, o3 = D_ATT, 2 * D_ATT, 3 * D_ATT
    o4 = o3 + N_HEADS
    row2 = lambda v: v.reshape(1, -1)

    kt_pages = cache_k.transpose(0, 1, 3, 4, 2).reshape(-1, depth, D_ATT, page)
    vt_pages = cache_v.transpose(0, 1, 3, 4, 2).reshape(-1, depth, D_ATT, page)
    lf_pages = cache_logf.transpose(0, 1, 3, 2)

    meta = jnp.broadcast_to(meta_tokens[None], (bp, N_META, d_model))
    xp = jnp.concatenate([meta, x_prompt, jnp.zeros((bp, lp - length, d_model), F32)], axis=1)
    xs = x_sample.reshape(n_s, d_model)
    wr_t = w_router.T
    br = b_router.reshape(N_EXPERTS, 1)
    zero_state = jnp.zeros((bp, 1, D_STATE), F32)
    zero_hist = jnp.zeros((bp, POOL_HIST, D_POOL), F32)

    outs_p = {k: [] for k in ("k", "v", "lf", "sr", "si", "pb")}
    outs_s = {k: [] for k in ("k", "v", "lf", "sr", "si", "pb")}
    for l in range(depth):
        w = w_in[l]
        wb = w.astype(BF16)
        wq = wb[:, :o1]
        wup = wb[:, o4:]
        wt = jnp.concatenate([wb[:, o1:o3], wb[:, o3:o4]], axis=1).T
        w_nat = jnp.concatenate([wb[:, :o3], wb[:, o4:]], axis=1)
        wf_t = wb[:, o3:o4].T
        bf = b_forget[l].reshape(N_HEADS, 1)

        apow_re, apow_im, bbr_t, bbi_t = _s5_params(ssm_lam_re[l], ssm_lam_im[l], ssm_log_dt[l],
                                                    ssm_b_re[l], ssm_b_im[l])
        apr = apow_re.reshape(SUBLANES, D_STATE)
        api = apow_im.reshape(SUBLANES, D_STATE)
        ridx = jnp.arange(SUBLANES)[:, None]
        ak = []
        for k in (1, 2, 4):
            ak.append(jnp.where(ridx >= k, apr[k - 1][None], 0.0))
            ak.append(jnp.where(ridx >= k, api[k - 1][None], 0.0))
        ak += [apr, api]
        mix_w = dict(
            bbr=_block_diag(bbr_t).astype(BF16), bbi=_block_diag(bbi_t).astype(BF16),
            cr=_block_diag(ssm_c_re[l].transpose(0, 2, 1)).astype(BF16),
            ci=_block_diag(ssm_c_im[l].transpose(0, 2, 1)).astype(BF16),
            d=row2(ssm_d[l]), ak=jnp.stack(ak), wglu=w_glu[l].astype(BF16), bglu=row2(b_glu[l]),
            wpool=_block_diag(w_pool[l]).astype(BF16), spool=row2(s_pool[l]))
        d_expert = w_gate.shape[-1]
        lw = dict(
            wo=w_out[l].astype(BF16), ln1_g=row2(ln1_g[l]), ln1_b=row2(ln1_b[l]), wr_t=wr_t, br=br,
            wg=w_gate[l].transpose(1, 0, 2).reshape(d_model, N_EXPERTS * d_expert).astype(BF16),
            wu=w_up[l].transpose(1, 0, 2).reshape(d_model, N_EXPERTS * d_expert).astype(BF16),
            wd=w_down[l].reshape(N_EXPERTS * d_expert, d_model).astype(BF16),
            ln2_g=row2(ln2_g[l]), ln2_b=row2(ln2_b[l]))

        res = _inproj_prompt(xp, row2(ln0_g), row2(ln0_b), wq, wup, wt, bf, l == 0, tile)
        if l == 0:
            xp, *res = res
        q, kt, vt, lft, ckt, u, p = res
        att = _attn_prompt(q, kt, vt, ckt, tile)
        ys, yq, sr, si = _seqmix(u, p, zero_state, zero_state, zero_hist, mix_w, length, 0, tile)
        xp = _post_block(xp.reshape(bp * lp, d_model), att.reshape(bp * lp, D_ATT),
                         ys.reshape(bp * lp, D_SSM), yq.reshape(bp * lp, D_POOL), lw, alpha,
                         tile).reshape(bp, lp, d_model)
        outs_p["k"].append(kt[:, :, :length])
        outs_p["v"].append(vt[:, :, :length])
        outs_p["lf"].append(lft[:, :, :length])
        outs_p["sr"].append(sr.reshape(bp, N_SSM_GROUPS, SSM_STATE))
        outs_p["si"].append(si.reshape(bp, N_SSM_GROUPS, SSM_STATE))
        outs_p["pb"].append(p[:, length - POOL_BUF:length])

        res = _inproj_sample(xs, row2(ln0_g), row2(ln0_b), w_nat, wf_t, bf, l == 0, tm_s)
        if l == 0:
            xs, *res = res
        qkv, up, lfs_t = res
        lf_new = lfs_t.reshape(N_HEADS, bs, t_new).transpose(1, 0, 2)
        lf_new_pad = jnp.pad(lf_new, ((0, 0), (0, 0), (0, LANES - t_new)))
        att = _attn_decode(page_table, qkv.reshape(bs, t_new, 3 * D_ATT), lf_new_pad,
                           kt_pages, vt_pages, lf_pages, l)
        hist = jnp.concatenate([jnp.zeros((bs, 1, D_POOL), F32), state_pool[:, l]], axis=1)
        up3 = up.reshape(bs, t_new, D_SSM + D_POOL)
        p_s = up3[..., D_SSM:]
        ys, yq, sr, si = _seqmix(up3[..., :D_SSM], p_s, state_ssm_re[:, l].reshape(bs, 1, D_STATE),
                                 state_ssm_im[:, l].reshape(bs, 1, D_STATE), hist, mix_w, t_new, past_len,
                                 t_new)
        xs = _post_block(xs, att.reshape(n_s, D_ATT), ys.reshape(n_s, D_SSM), yq.reshape(n_s, D_POOL),
                         lw, alpha, tm_s)
        qkv3 = qkv.reshape(bs, t_new, 3, N_HEADS, HEAD_DIM)
        outs_s["k"].append(qkv3[:, :, 1])
        outs_s["v"].append(qkv3[:, :, 2])
        outs_s["lf"].append(lf_new.transpose(0, 2, 1))
        outs_s["sr"].append(sr.reshape(bs, N_SSM_GROUPS, SSM_STATE))
        outs_s["si"].append(si.reshape(bs, N_SSM_GROUPS, SSM_STATE))
        outs_s["pb"].append(jnp.concatenate([state_pool[:, l], p_s], axis=1)[:, -POOL_BUF:])

    def heads_last(xs_t):
        a = jnp.stack(xs_t, axis=1)
        b_, dep, _, ln = a.shape
        return a.reshape(b_, dep, N_HEADS, HEAD_DIM, ln).transpose(0, 1, 4, 2, 3)

    y_prompt = xp[:, N_META:length]
    y_sample = xs.reshape(bs, t_new, d_model)
    return (y_prompt, y_sample,
            heads_last(outs_p["k"]), heads_last(outs_p["v"]),
            jnp.stack(outs_p["lf"], axis=1).transpose(0, 1, 3, 2),
            jnp.stack(outs_p["sr"], axis=1), jnp.stack(outs_p["si"], axis=1), jnp.stack(outs_p["pb"], axis=1),
            jnp.stack(outs_s["k"], axis=1), jnp.stack(outs_s["v"], axis=1), jnp.stack(outs_s["lf"], axis=1),
            jnp.stack(outs_s["sr"], axis=1), jnp.stack(outs_s["si"], axis=1), jnp.stack(outs_s["pb"], axis=1))
```

```python
import functools
import math

import jax
import jax.numpy as jnp
from jax import lax
from jax.experimental import pallas as pl
from jax.experimental.pallas import tpu as pltpu

F32 = jnp.float32
BF16 = jnp.bfloat16

N_META = 16
N_HEADS = 8
HEAD_DIM = 64
D_ATT = N_HEADS * HEAD_DIM
SSM_GROUP = 16
N_SSM_GROUPS = 16
SSM_STATE = 64
D_SSM = SSM_GROUP * N_SSM_GROUPS
D_STATE = N_SSM_GROUPS * SSM_STATE
POOL_WINDOWS = (2, 4, 8, 16)
POOL_GROUP = 64
D_POOL = POOL_GROUP * len(POOL_WINDOWS)
POOL_BUF = max(POOL_WINDOWS) - 1
POOL_HIST = POOL_BUF + 1
N_EXPERTS = 16
N_EXPERT_GROUPS = 4
EXPERTS_PER_GROUP = N_EXPERTS // N_EXPERT_GROUPS
TOP_K = 2
SCALE = HEAD_DIM ** -0.5
LN_EPS = 1e-5
NEG_INF = -1e30
HEADS_PER_STEP = 2
BIAS_PIECES = 3
SOFTMAX_ROWS = 32
DECODE_SEQS_PER_STEP = 2

LANES = 128
SUBLANES = 8
VMEM_LIMIT = 56 * 1024 * 1024


def _cparams(n_axes):
    return pltpu.CompilerParams(
        dimension_semantics=("arbitrary",) * n_axes, vmem_limit_bytes=VMEM_LIMIT)


def _round_up(x, m):
    return (x + m - 1) // m * m


def _pick_tile(n, candidates):
    for c in candidates:
        if n % c == 0:
            return c
    raise ValueError(f"no tile for {n}")


def _layer_norm(x, g, b):
    mu = jnp.mean(x, axis=-1, keepdims=True)
    xc = x - mu
    var = jnp.mean(xc * xc, axis=-1, keepdims=True)
    return xc * lax.rsqrt(var + LN_EPS) * g + b


def _log_sigmoid(x):
    return -(jnp.maximum(-x, 0.0) + jnp.log1p(jnp.exp(-jnp.abs(x))))


def _dot(a, b):
    return jnp.dot(a, b, preferred_element_type=F32)


def _dot_nt(a, b, precision=None):
    return lax.dot_general(a, b, (((1,), (1,)), ((), ())),
                           preferred_element_type=F32, precision=precision)


def _lane_cumsum(x):
    lane = lax.broadcasted_iota(jnp.int32, x.shape, 1)
    s = 1
    while s < LANES:
        x = x + jnp.where(lane >= s, pltpu.roll(x, s, axis=1), 0.0)
        s *= 2
    return x


def _s5_param_kernel(lr_ref, li_ref, ldt_ref, br_ref, bi_ref, apow_re, apow_im, bbr_ref, bbi_ref):
    lr = lr_ref[...]
    li = li_ref[...]
    dt = jnp.exp(ldt_ref[...])
    mag = jnp.exp(lr * dt)
    ab_re = mag * jnp.cos(li * dt)
    ab_im = mag * jnp.sin(li * dt)
    nr, ni = ab_re - 1.0, ab_im
    den = lr * lr + li * li
    gr = (nr * lr + ni * li) / den
    gi = (ni * lr - nr * li) / den
    br = br_ref[...]
    bi = bi_ref[...]
    bbr_ref[...] = gr[:, None, :] * br - gi[:, None, :] * bi
    bbi_ref[...] = gr[:, None, :] * bi + gi[:, None, :] * br
    pr, pi = ab_re, ab_im
    apow_re[0] = pr
    apow_im[0] = pi
    for k in range(1, SUBLANES):
        pr, pi = pr * ab_re - pi * ab_im, pr * ab_im + pi * ab_re
        apow_re[k] = pr
        apow_im[k] = pi


def _s5_params(lam_re, lam_im, log_dt, b_re, b_im):
    g, p = lam_re.shape
    c = b_re.shape[-1]
    out_shape = (jax.ShapeDtypeStruct((SUBLANES, g, p), F32), jax.ShapeDtypeStruct((SUBLANES, g, p), F32),
                 jax.ShapeDtypeStruct((g, c, p), F32), jax.ShapeDtypeStruct((g, c, p), F32))
    return pl.pallas_call(_s5_param_kernel, out_shape=out_shape, name="s5_params")(
        lam_re, lam_im, log_dt.reshape(g, 1), b_re.transpose(0, 2, 1), b_im.transpose(0, 2, 1))


def _block_diag(w):
    g, a, b = w.shape
    eye = jnp.eye(g, dtype=w.dtype)
    return (w[:, :, None, :] * eye[:, None, :, None]).reshape(g * a, g * b)


def _inproj_prompt_kernel(apply_ln, tm, x_ref, g_ref, b_ref, wq_ref, wup_ref, wt_ref, bf_ref, *outs):
    if apply_ln:
        xn_ref, q_ref, kt_ref, vt_ref, ka_ref, vb_ref, lft_ref, u_ref, p_ref, carry_ref, ck_scr = outs
    else:
        q_ref, kt_ref, vt_ref, ka_ref, vb_ref, lft_ref, u_ref, p_ref, carry_ref, ck_scr = outs
    t = pl.program_id(1)
    x = x_ref[0]
    if apply_ln:
        x = _layer_norm(x, g_ref[...], b_ref[...])
        xn_ref[0] = x
    xb = x.astype(BF16)
    q = _dot(xb, wq_ref[...]) * SCALE
    lane = lax.broadcasted_iota(jnp.int32, (tm, LANES), 1)
    ones = jnp.where(lane < HEAD_DIM + BIAS_PIECES, 1.0, 0.0)
    for h in range(N_HEADS):
        pair = q[:, (h // 2) * LANES:(h // 2 + 1) * LANES]
        if h % 2:
            pair = pltpu.roll(pair, HEAD_DIM, axis=1)
        q_ref[0, h] = jnp.where(lane < HEAD_DIM, pair, ones).astype(BF16)
    up = _dot(xb, wup_ref[...])
    u_ref[0] = up[:, :D_SSM]
    p_ref[0] = up[:, D_SSM:]
    zt = _dot_nt(wt_ref[...], xb)
    kt = zt[:D_ATT]
    vt = zt[D_ATT:2 * D_ATT]
    kt_ref[0] = kt
    vt_ref[0] = vt
    vb_ref[0] = vt.astype(BF16)
    lf = _log_sigmoid(zt[2 * D_ATT:] + bf_ref[...])
    lft_ref[0] = lf

    @pl.when(t == 0)
    def _():
        carry_ref[...] = jnp.zeros_like(carry_ref)

    carry = carry_ref[...]
    for c in range(tm // LANES):
        blk = _lane_cumsum(lf[:, c * LANES:(c + 1) * LANES]) + carry
        ck_scr[:, c * LANES:(c + 1) * LANES] = blk
        carry = jnp.broadcast_to(blk[:, LANES - 1:LANES], carry.shape)
    carry_ref[...] = carry

    piece_row = lax.broadcasted_iota(jnp.int32, (HEAD_DIM, tm), 0)
    for h in range(N_HEADS):
        ka_ref[0, h, 0:HEAD_DIM, :] = kt[h * HEAD_DIM:(h + 1) * HEAD_DIM].astype(BF16)
        rest = -ck_scr[h:h + 1, :]
        aug = jnp.zeros((HEAD_DIM, tm), F32)
        for i in range(BIAS_PIECES):
            piece = rest.astype(BF16).astype(F32)
            aug = jnp.where(piece_row == i, piece, aug)
            rest = rest - piece
        ka_ref[0, h, HEAD_DIM:, :] = aug.astype(BF16)


def _inproj_prompt(x, ln_g, ln_b, wq, wup, wt, bf, apply_ln, tm):
    b, lp, d = x.shape
    nt = lp // tm
    full = lambda shape: pl.BlockSpec(shape, lambda i, j: (0,) * len(shape))
    out_shape, out_specs = [], []
    if apply_ln:
        out_shape.append(jax.ShapeDtypeStruct((b, lp, d), F32))
        out_specs.append(pl.BlockSpec((1, tm, d), lambda i, j: (i, j, 0)))
    out_shape += [
        jax.ShapeDtypeStruct((b, N_HEADS, lp, LANES), BF16),
        jax.ShapeDtypeStruct((b, D_ATT, lp), F32),
        jax.ShapeDtypeStruct((b, D_ATT, lp), F32),
        jax.ShapeDtypeStruct((b, N_HEADS, LANES, lp), BF16),
        jax.ShapeDtypeStruct((b, D_ATT, lp), BF16),
        jax.ShapeDtypeStruct((b, N_HEADS, lp), F32),
        jax.ShapeDtypeStruct((b, lp, D_SSM), F32),
        jax.ShapeDtypeStruct((b, lp, D_POOL), F32),
    ]
    out_specs += [
        pl.BlockSpec((1, N_HEADS, tm, LANES), lambda i, j: (i, 0, j, 0)),
        pl.BlockSpec((1, D_ATT, tm), lambda i, j: (i, 0, j)),
        pl.BlockSpec((1, D_ATT, tm), lambda i, j: (i, 0, j)),
        pl.BlockSpec((1, N_HEADS, LANES, tm), lambda i, j: (i, 0, 0, j)),
        pl.BlockSpec((1, D_ATT, tm), lambda i, j: (i, 0, j)),
        pl.BlockSpec((1, N_HEADS, tm), lambda i, j: (i, 0, j)),
        pl.BlockSpec((1, tm, D_SSM), lambda i, j: (i, j, 0)),
        pl.BlockSpec((1, tm, D_POOL), lambda i, j: (i, j, 0)),
    ]
    return pl.pallas_call(
        functools.partial(_inproj_prompt_kernel, apply_ln, tm),
        grid=(b, nt),
        in_specs=[pl.BlockSpec((1, tm, d), lambda i, j: (i, j, 0)),
                  full(ln_g.shape), full(ln_b.shape), full(wq.shape), full(wup.shape),
                  full(wt.shape), full(bf.shape)],
        out_specs=out_specs,
        out_shape=out_shape,
        scratch_shapes=[pltpu.VMEM((N_HEADS, LANES), F32), pltpu.VMEM((N_HEADS, tm), F32)],
        compiler_params=_cparams(2),
        name="inproj_prompt",
    )(x, ln_g, ln_b, wq, wup, wt, bf)


def _inproj_sample_kernel(apply_ln, x_ref, g_ref, b_ref, w_ref, wf_ref, bf_ref, *outs):
    if apply_ln:
        xn_ref, qkv_ref, up_ref, lft_ref = outs
    else:
        qkv_ref, up_ref, lft_ref = outs
    x = x_ref[...]
    if apply_ln:
        x = _layer_norm(x, g_ref[...], b_ref[...])
        xn_ref[...] = x
    xb = x.astype(BF16)
    z = _dot(xb, w_ref[...])
    qkv_ref[...] = z[:, :3 * D_ATT]
    up_ref[...] = z[:, 3 * D_ATT:]
    lft_ref[...] = _log_sigmoid(_dot_nt(wf_ref[...], xb) + bf_ref[...])


def _inproj_sample(x, ln_g, ln_b, w, wf, bf, apply_ln, tm):
    n, d = x.shape
    full = lambda shape: pl.BlockSpec(shape, lambda i: (0,) * len(shape))
    out_shape, out_specs = [], []
    if apply_ln:
        out_shape.append(jax.ShapeDtypeStruct((n, d), F32))
        out_specs.append(pl.BlockSpec((tm, d), lambda i: (i, 0)))
    out_shape += [jax.ShapeDtypeStruct((n, 3 * D_ATT), F32),
                  jax.ShapeDtypeStruct((n, D_SSM + D_POOL), F32),
                  jax.ShapeDtypeStruct((N_HEADS, n), F32)]
    out_specs += [pl.BlockSpec((tm, 3 * D_ATT), lambda i: (i, 0)),
                  pl.BlockSpec((tm, D_SSM + D_POOL), lambda i: (i, 0)),
                  pl.BlockSpec((N_HEADS, tm), lambda i: (0, i))]
    return pl.pallas_call(
        functools.partial(_inproj_sample_kernel, apply_ln),
        grid=(n // tm,),
        in_specs=[pl.BlockSpec((tm, d), lambda i: (i, 0)),
                  full(ln_g.shape), full(ln_b.shape), full(w.shape), full(wf.shape), full(bf.shape)],
        out_specs=out_specs,
        out_shape=out_shape,
        compiler_params=_cparams(1),
        name="inproj_sample",
    )(x, ln_g, ln_b, w, wf, bf)


def _attn_prompt_kernel(tq, q_ref, ka_ref, vb_ref, o_ref, s_scr, p_scr, m_scr, l_scr, a_scr, acc_scr):
    qi = pl.program_id(2)
    rc = SOFTMAX_ROWS
    rep = tq // LANES
    m_scr[...] = jnp.full(m_scr.shape, NEG_INF, F32)
    l_scr[...] = jnp.zeros(l_scr.shape, F32)
    acc_scr[...] = jnp.zeros(acc_scr.shape, F32)
    row = lax.broadcasted_iota(jnp.int32, (rc, tq), 0)
    col = lax.broadcasted_iota(jnp.int32, (rc, tq), 1)

    def block(ki, masked):
        k0 = pl.multiple_of(ki * tq, LANES)
        for j in range(HEADS_PER_STEP):
            s_scr[j] = _dot(q_ref[0, j], ka_ref[0, j, :, pl.ds(k0, tq)])
        for j in range(HEADS_PER_STEP):
            for c in range(tq // rc):
                rows = slice(c * rc, (c + 1) * rc)
                sc = s_scr[j, rows, :]
                if masked:
                    sc = jnp.where(col <= row + c * rc, sc, NEG_INF)
                m_old = m_scr[j, rows, :]
                m_new = jnp.maximum(m_old, jnp.max(sc, axis=1, keepdims=True))
                alpha = jnp.exp(m_old - m_new)
                p = jnp.exp(sc - pltpu.repeat(m_new, rep, axis=1))
                l_scr[j, rows, :] = alpha * l_scr[j, rows, :] + jnp.sum(p, axis=1, keepdims=True)
                m_scr[j, rows, :] = m_new
                a_scr[j, rows, :] = alpha
                p_scr[j, rows, :] = p.astype(BF16)
        for j in range(HEADS_PER_STEP):
            hs = slice(j * HEAD_DIM, (j + 1) * HEAD_DIM)
            acc_scr[j] = (a_scr[j, :, 0:HEAD_DIM] * acc_scr[j]
                          + _dot_nt(p_scr[j], vb_ref[0, hs, pl.ds(k0, tq)]))

    def body(ki, carry):
        block(ki, False)
        return carry

    lax.fori_loop(0, qi, body, 0)
    block(qi, True)
    o_ref[0] = jnp.concatenate([acc_scr[j] / l_scr[j, :, 0:HEAD_DIM] for j in range(HEADS_PER_STEP)],
                               axis=1).astype(o_ref.dtype)


def _attn_prompt(q, ka, vb, tq):
    b, _, lp, _ = q.shape
    hp = N_HEADS // HEADS_PER_STEP
    rows = HEADS_PER_STEP * HEAD_DIM
    return pl.pallas_call(
        functools.partial(_attn_prompt_kernel, tq),
        grid=(b, hp, lp // tq),
        in_specs=[pl.BlockSpec((1, HEADS_PER_STEP, tq, LANES), lambda i, h, j: (i, h, j, 0)),
                  pl.BlockSpec((1, HEADS_PER_STEP, LANES, lp), lambda i, h, j: (i, h, 0, 0)),
                  pl.BlockSpec((1, rows, lp), lambda i, h, j: (i, h, 0))],
        out_specs=pl.BlockSpec((1, tq, rows), lambda i, h, j: (i, j, h)),
        out_shape=jax.ShapeDtypeStruct((b, lp, D_ATT), BF16),
        scratch_shapes=[pltpu.VMEM((HEADS_PER_STEP, tq, tq), F32), pltpu.VMEM((HEADS_PER_STEP, tq, tq), BF16),
                        pltpu.VMEM((HEADS_PER_STEP, tq, LANES), F32), pltpu.VMEM((HEADS_PER_STEP, tq, LANES), F32),
                        pltpu.VMEM((HEADS_PER_STEP, tq, LANES), F32),
                        pltpu.VMEM((HEADS_PER_STEP, tq, HEAD_DIM), F32)],
        compiler_params=_cparams(3),
        name="attn_prompt",
    )(q, ka, vb)


def _attn_decode_kernel(n_pages, t_new, pt_ref, qkv_ref, lfn_ref, *refs):
    o_ref = refs[-1]
    rows = t_new * N_HEADS
    head_of_row = lax.broadcasted_iota(jnp.int32, (N_HEADS, D_ATT), 0)
    head_of_col = lax.broadcasted_iota(jnp.int32, (N_HEADS, D_ATT), 1) // HEAD_DIM
    hmask = head_of_row == head_of_col
    omask = jnp.concatenate([hmask] * t_new, axis=0)
    key_t = lax.broadcasted_iota(jnp.int32, (rows, LANES), 1)
    qry_t = lax.broadcasted_iota(jnp.int32, (rows, LANES), 0) // N_HEADS
    pad = jnp.zeros((LANES - t_new, D_ATT), F32)

    def expand(c):
        return jnp.concatenate([c] * t_new, axis=0)

    for sq in range(DECODE_SEQS_PER_STEP):
        base = 3 * n_pages * sq
        k_refs = refs[base:base + n_pages]
        v_refs = refs[base + n_pages:base + 2 * n_pages]
        lf_refs = refs[base + 2 * n_pages:base + 3 * n_pages]
        qkv = qkv_ref[sq]
        q = qkv[:, :D_ATT] * SCALE
        k_new = qkv[:, D_ATT:2 * D_ATT]
        v_new = qkv[:, 2 * D_ATT:]
        qbd = jnp.concatenate(
            [jnp.where(hmask, jnp.broadcast_to(q[t:t + 1], (N_HEADS, D_ATT)), 0.0) for t in range(t_new)],
            axis=0).astype(BF16)

        local = _lane_cumsum(jnp.concatenate([lf_refs[j][0, 0] for j in range(n_pages)] + [lfn_ref[sq]],
                                             axis=0))
        off = jnp.zeros((N_HEADS, LANES), F32)
        s_blocks = []
        for j in range(n_pages):
            ck = local[j * N_HEADS:(j + 1) * N_HEADS] + off
            off = jnp.broadcast_to(ck[:, LANES - 1:LANES], off.shape)
            s = _dot(qbd, k_refs[j][0, 0].astype(BF16))
            s_blocks.append(s - expand(ck))
        k_pad = jnp.concatenate([k_new, pad], axis=0).astype(BF16)
        v_pad = jnp.concatenate([v_new, pad], axis=0).astype(BF16)
        ck_new = local[n_pages * N_HEADS:] + off
        s_new = _dot_nt(qbd, k_pad) - expand(ck_new)
        s_blocks.append(jnp.where(key_t <= qry_t, s_new, NEG_INF))

        m = s_blocks[0]
        for s in s_blocks[1:]:
            m = jnp.maximum(m, s)
        m = m.max(axis=1, keepdims=True)
        l = jnp.zeros((rows, LANES), F32)
        acc = jnp.zeros((rows, D_ATT), F32)
        for j, s in enumerate(s_blocks):
            p = jnp.exp(s - m)
            l = l + p
            if j < n_pages:
                acc = acc + _dot_nt(p.astype(BF16), v_refs[j][0, 0].astype(BF16))
            else:
                acc = acc + _dot(p.astype(BF16), v_pad)
        acc = acc / l.sum(axis=1, keepdims=True)
        acc = jnp.where(omask, acc, 0.0)
        o_ref[sq] = acc.reshape(t_new, N_HEADS, D_ATT).sum(axis=1).astype(o_ref.dtype)


def _attn_decode(page_table, qkv, lf_new_t, kt_pages, vt_pages, lf_pages, layer):
    bs, t_new, _ = qkv.shape
    n_pages = page_table.shape[1]
    page = kt_pages.shape[-1]
    sps = DECODE_SEQS_PER_STEP
    pt_flat = page_table.reshape(-1)

    def page_spec(rows, sq, j):
        return pl.BlockSpec((1, 1, rows, page), lambda i, pt: (pt[(i * sps + sq) * n_pages + j], layer, 0, 0))

    in_specs = [pl.BlockSpec((sps, t_new, 3 * D_ATT), lambda i, pt: (i, 0, 0)),
                pl.BlockSpec((sps, N_HEADS, LANES), lambda i, pt: (i, 0, 0))]
    operands = []
    for sq in range(sps):
        in_specs += [page_spec(D_ATT, sq, j) for j in range(n_pages)]
        in_specs += [page_spec(D_ATT, sq, j) for j in range(n_pages)]
        in_specs += [page_spec(N_HEADS, sq, j) for j in range(n_pages)]
        operands += [kt_pages] * n_pages + [vt_pages] * n_pages + [lf_pages] * n_pages
    grid_spec = pltpu.PrefetchScalarGridSpec(
        num_scalar_prefetch=1, grid=(bs // sps,), in_specs=in_specs,
        out_specs=pl.BlockSpec((sps, t_new, D_ATT), lambda i, pt: (i, 0, 0)))
    return pl.pallas_call(
        functools.partial(_attn_decode_kernel, n_pages, t_new),
        grid_spec=grid_spec,
        out_shape=jax.ShapeDtypeStruct((bs, t_new, D_ATT), BF16),
        compiler_params=_cparams(1),
        name="attn_decode",
    )(pt_flat, qkv, lf_new_t, *operands)


def _s5_scan_rows(br, bi, cr, ci, ak_ref):
    for idx, k in enumerate((1, 2, 4)):
        akr = ak_ref[2 * idx]
        aki = ak_ref[2 * idx + 1]
        sr = pltpu.roll(br, k, axis=0)
        si = pltpu.roll(bi, k, axis=0)
        br, bi = br + akr * sr - aki * si, bi + akr * si + aki * sr
    pr = ak_ref[6]
    pi = ak_ref[7]
    return br + pr * cr - pi * ci, bi + pr * ci + pi * cr


def _s5_readout(xr, xi, u, cr_ref, ci_ref, d_ref, wglu_ref, bglu_ref):
    y = _dot(xr.astype(BF16), cr_ref[...]) - _dot(xi.astype(BF16), ci_ref[...]) + d_ref[...] * u
    y = jax.nn.gelu(y)
    return y * jax.nn.sigmoid(_dot(y.astype(BF16), wglu_ref[...]) + bglu_ref[...])


def _pool_window_sums(e):
    c2 = e + pltpu.roll(e, 1, axis=0)
    c4 = c2 + pltpu.roll(c2, 2, axis=0)
    c8 = c4 + pltpu.roll(c4, 4, axis=0)
    c16 = c8 + pltpu.roll(c8, 8, axis=0)
    return c2, c4, c8, c16


def _pool_mix(sums, tok, pos, wpool_ref, spool_ref):
    c2, c4, c8, c16 = sums
    grp = lax.broadcasted_iota(jnp.int32, tok.shape, 1) // POOL_GROUP
    win = jnp.where(grp == 0, c2, jnp.where(grp == 1, c4, jnp.where(grp == 2, c8, c16)))
    wsize = jnp.where(grp == 0, 2, jnp.where(grp == 1, 4, jnp.where(grp == 2, 8, 16)))
    cnt = jnp.minimum(wsize, pos + 1).astype(F32)
    mm = win / cnt - tok
    return _dot(mm.astype(BF16), wpool_ref[...]) * spool_ref[...]


def _seqmix_kernel(tt, snap_tile, snap_row,
                   u_ref, p_ref, bbr_ref, bbi_ref, cr_ref, ci_ref, d_ref, ak_ref, wglu_ref, bglu_ref,
                   wpool_ref, spool_ref,
                   ys_ref, yq_ref, sr_ref, si_ref,
                   xr_scr, xi_scr, car_r, car_i, ext_scr):
    t = pl.program_id(1)

    @pl.when(t == 0)
    def _():
        car_r[...] = jnp.zeros_like(car_r)
        car_i[...] = jnp.zeros_like(car_i)
        ext_scr[0:POOL_HIST, :] = jnp.zeros((POOL_HIST, D_POOL), F32)

    u = u_ref[0]
    ub = u.astype(BF16)
    xr_scr[...] = _dot(ub, bbr_ref[...])
    xi_scr[...] = _dot(ub, bbi_ref[...])

    def scan_block(i, carry):
        cr, ci = carry
        r0 = pl.multiple_of(i * SUBLANES, SUBLANES)
        xr, xi = _s5_scan_rows(xr_scr[pl.ds(r0, SUBLANES), :], xi_scr[pl.ds(r0, SUBLANES), :], cr, ci, ak_ref)
        xr_scr[pl.ds(r0, SUBLANES), :] = xr
        xi_scr[pl.ds(r0, SUBLANES), :] = xi
        return xr[SUBLANES - 1:SUBLANES], xi[SUBLANES - 1:SUBLANES]

    cr, ci = lax.fori_loop(0, tt // SUBLANES, scan_block, (car_r[...], car_i[...]))
    car_r[...] = cr
    car_i[...] = ci

    @pl.when(t == snap_tile)
    def _():
        sr_ref[0] = xr_scr[snap_row:snap_row + 1, :]
        si_ref[0] = xi_scr[snap_row:snap_row + 1, :]

    ys_ref[0] = _s5_readout(xr_scr[...], xi_scr[...], u, cr_ref, ci_ref, d_ref, wglu_ref,
                            bglu_ref).astype(ys_ref.dtype)

    ext_scr[POOL_HIST:POOL_HIST + tt, :] = p_ref[0]
    e = ext_scr[...]
    sums = [c[POOL_HIST:] for c in _pool_window_sums(e)]
    pos = t * tt + lax.broadcasted_iota(jnp.int32, (tt, D_POOL), 0)
    yq_ref[0] = _pool_mix(sums, e[POOL_HIST:], pos, wpool_ref, spool_ref).astype(yq_ref.dtype)
    ext_scr[0:POOL_HIST, :] = e[tt:tt + POOL_HIST]


def _mix_weights(prm):
    return (prm["bbr"], prm["bbi"], prm["cr"], prm["ci"], prm["d"], prm["ak"], prm["wglu"],
            prm["bglu"], prm["wpool"], prm["spool"])


def _seqmix(u, p, prm, length, tt):
    b, lp, _ = u.shape
    nt = lp // tt
    snap_tile, snap_row = (length - 1) // tt, (length - 1) % tt
    full = lambda a: pl.BlockSpec(a.shape, lambda i, j: (0,) * a.ndim)
    weights = _mix_weights(prm)
    return pl.pallas_call(
        functools.partial(_seqmix_kernel, tt, snap_tile, snap_row),
        grid=(b, nt),
        in_specs=[pl.BlockSpec((1, tt, D_SSM), lambda i, j: (i, j, 0)),
                  pl.BlockSpec((1, tt, D_POOL), lambda i, j: (i, j, 0))]
                 + [full(w) for w in weights],
        out_specs=[pl.BlockSpec((1, tt, D_SSM), lambda i, j: (i, j, 0)),
                   pl.BlockSpec((1, tt, D_POOL), lambda i, j: (i, j, 0)),
                   pl.BlockSpec((1, 1, D_STATE), lambda i, j: (i, 0, 0)),
                   pl.BlockSpec((1, 1, D_STATE), lambda i, j: (i, 0, 0))],
        out_shape=[jax.ShapeDtypeStruct((b, lp, D_SSM), BF16),
                   jax.ShapeDtypeStruct((b, lp, D_POOL), BF16),
                   jax.ShapeDtypeStruct((b, 1, D_STATE), F32),
                   jax.ShapeDtypeStruct((b, 1, D_STATE), F32)],
        scratch_shapes=[pltpu.VMEM((tt, D_STATE), F32), pltpu.VMEM((tt, D_STATE), F32),
                        pltpu.VMEM((1, D_STATE), F32), pltpu.VMEM((1, D_STATE), F32),
                        pltpu.VMEM((POOL_HIST + tt, D_POOL), F32)],
        compiler_params=_cparams(2),
        name="seqmix",
    )(u, p, *weights)


def _seqmix_sample_kernel(nb, pos0, up_ref, x0r_ref, x0i_ref, hist_ref,
                          bbr_ref, bbi_ref, cr_ref, ci_ref, d_ref, ak_ref, wglu_ref, bglu_ref,
                          wpool_ref, spool_ref,
                          ys_ref, yq_ref, sr_ref, si_ref, xr_scr, xi_scr):
    t_new = SUBLANES
    up = up_ref[...].reshape(nb * t_new, D_SSM + D_POOL)
    u = up[:, :D_SSM]
    ub = u.astype(BF16)
    xr_scr[...] = _dot(ub, bbr_ref[...])
    xi_scr[...] = _dot(ub, bbi_ref[...])

    def scan_block(i, carry):
        r0 = pl.multiple_of(i * SUBLANES, SUBLANES)
        xr, xi = _s5_scan_rows(xr_scr[pl.ds(r0, SUBLANES), :], xi_scr[pl.ds(r0, SUBLANES), :],
                               x0r_ref[i], x0i_ref[i], ak_ref)
        xr_scr[pl.ds(r0, SUBLANES), :] = xr
        xi_scr[pl.ds(r0, SUBLANES), :] = xi
        sr_ref[i] = xr[SUBLANES - 1:SUBLANES]
        si_ref[i] = xi[SUBLANES - 1:SUBLANES]
        return carry

    lax.fori_loop(0, nb, scan_block, 0)
    ys = _s5_readout(xr_scr[...], xi_scr[...], u, cr_ref, ci_ref, d_ref, wglu_ref, bglu_ref)
    ys_ref[...] = ys.reshape(nb, t_new, D_SSM).astype(ys_ref.dtype)

    tok = up[:, D_SSM:].reshape(nb, t_new, D_POOL)
    per_seq = POOL_HIST + t_new
    e = jnp.concatenate([hist_ref[...], tok], axis=1).reshape(nb * per_seq, D_POOL)
    sums = [c.reshape(nb, per_seq, D_POOL)[:, POOL_HIST:].reshape(nb * t_new, D_POOL)
            for c in _pool_window_sums(e)]
    pos = pos0 + lax.broadcasted_iota(jnp.int32, (nb, t_new, D_POOL), 1).reshape(nb * t_new, D_POOL)
    yq = _pool_mix(sums, up[:, D_SSM:], pos, wpool_ref, spool_ref)
    yq_ref[...] = yq.reshape(nb, t_new, D_POOL).astype(yq_ref.dtype)


def _seqmix_sample(up, x0r, x0i, hist, prm, pos0, nb):
    bs, t_new, width = up.shape
    full = lambda a: pl.BlockSpec(a.shape, lambda i: (0,) * a.ndim)
    seq = lambda r, w: pl.BlockSpec((nb, r, w), lambda i: (i, 0, 0))
    weights = _mix_weights(prm)
    return pl.pallas_call(
        functools.partial(_seqmix_sample_kernel, nb, pos0),
        grid=(bs // nb,),
        in_specs=[seq(t_new, width), seq(1, D_STATE), seq(1, D_STATE), seq(POOL_HIST, D_POOL)]
                 + [full(w) for w in weights],
        out_specs=[seq(t_new, D_SSM), seq(t_new, D_POOL), seq(1, D_STATE), seq(1, D_STATE)],
        out_shape=[jax.ShapeDtypeStruct((bs, t_new, D_SSM), BF16),
                   jax.ShapeDtypeStruct((bs, t_new, D_POOL), BF16),
                   jax.ShapeDtypeStruct((bs, 1, D_STATE), F32),
                   jax.ShapeDtypeStruct((bs, 1, D_STATE), F32)],
        scratch_shapes=[pltpu.VMEM((nb * t_new, D_STATE), F32), pltpu.VMEM((nb * t_new, D_STATE), F32)],
        compiler_params=_cparams(1),
        name="seqmix_sample",
    )(up, x0r, x0i, hist, *weights)


def _outproj_kernel(alpha, x_ref, att_ref, ys_ref, yq_ref, wo_ref, g_ref, b_ref, wr_ref, br_ref,
                    h_ref, gates_ref):
    y = alpha * x_ref[...]
    y = y + _dot(att_ref[...], wo_ref[0:D_ATT, :])
    y = y + _dot(ys_ref[...], wo_ref[D_ATT:D_ATT + D_SSM, :])
    y = y + _dot(yq_ref[...], wo_ref[D_ATT + D_SSM:, :])
    h = _layer_norm(y, g_ref[...], b_ref[...])
    h_ref[...] = h
    logits = _dot_nt(wr_ref[...], h, precision=lax.Precision.HIGHEST)
    aff = jax.nn.sigmoid(logits)
    biased = aff + br_ref[...]
    rows = [biased[e:e + 1, :] for e in range(N_EXPERTS)]

    def beats(a, ia, c, ic):
        return (a >= c) if ia < ic else (a > c)

    gscore = []
    for g in range(N_EXPERT_GROUPS):
        members = list(range(g * EXPERTS_PER_GROUP, (g + 1) * EXPERTS_PER_GROUP))
        total = jnp.zeros_like(rows[0])
        for e in members:
            rank = jnp.zeros_like(rows[0])
            for o in members:
                if o != e:
                    rank = rank + beats(rows[o], o, rows[e], e).astype(F32)
            total = total + jnp.where(rank < 2.0, rows[e], 0.0)
        gscore.append(total)
    in_group = []
    for g in range(N_EXPERT_GROUPS):
        lost = jnp.zeros_like(rows[0])
        for o in range(N_EXPERT_GROUPS):
            if o != g:
                lost = lost + beats(gscore[o], o, gscore[g], g).astype(F32)
        in_group.append(lost < 1.0)
    masked = [jnp.where(in_group[e // EXPERTS_PER_GROUP], rows[e], NEG_INF) for e in range(N_EXPERTS)]
    sel = []
    for e in range(N_EXPERTS):
        rank = jnp.zeros_like(rows[0])
        for o in range(N_EXPERTS):
            if o != e:
                rank = rank + beats(masked[o], o, masked[e], e).astype(F32)
        sel.append(jnp.where(rank < float(TOP_K), aff[e:e + 1, :], 0.0))
    denom = sel[0]
    for e in range(1, N_EXPERTS):
        denom = denom + sel[e]
    gates_ref[...] = jnp.concatenate(sel, axis=0) / denom


def _outproj(x, att, ys, yq, wo, ln_g, ln_b, wr_t, br, alpha, tm):
    n, d = x.shape
    full = lambda a: pl.BlockSpec(a.shape, lambda i: (0,) * a.ndim)
    row = lambda w: pl.BlockSpec((tm, w), lambda i: (i, 0))
    return pl.pallas_call(
        functools.partial(_outproj_kernel, alpha),
        grid=(n // tm,),
        in_specs=[row(d), row(D_ATT), row(D_SSM), row(D_POOL), full(wo), full(ln_g), full(ln_b),
                  full(wr_t), full(br)],
        out_specs=[row(d), pl.BlockSpec((N_EXPERTS, tm), lambda i: (0, i))],
        out_shape=[jax.ShapeDtypeStruct((n, d), F32), jax.ShapeDtypeStruct((N_EXPERTS, n), F32)],
        compiler_params=_cparams(1),
        name="outproj_ln_router",
    )(x, att, ys, yq, wo, ln_g, ln_b, wr_t, br)


def _moe_kernel(alpha, d_expert, h_ref, gates_ref, wg_ref, wu_ref, wd_ref, g_ref, b_ref, o_ref,
                hb_ref, acc_ref):
    g = pl.program_id(1)

    @pl.when(g == 0)
    def _():
        hb_ref[...] = h_ref[...].astype(BF16)
        acc_ref[...] = jnp.zeros_like(acc_ref)

    hb = hb_ref[...]
    a = _dot(hb, wg_ref[...])
    b = _dot(hb, wu_ref[...])
    gates = gates_ref[0]
    parts = []
    for e in range(EXPERTS_PER_GROUP):
        cs = slice(e * d_expert, (e + 1) * d_expert)
        hid = jax.nn.silu(a[:, cs]) * b[:, cs] * gates[:, e:e + 1]
        parts.append(hid.astype(BF16))
    acc_ref[...] += _dot(jnp.concatenate(parts, axis=1), wd_ref[...])

    @pl.when(g == N_EXPERT_GROUPS - 1)
    def _():
        y = alpha * h_ref[...] + acc_ref[...]
        o_ref[...] = _layer_norm(y, g_ref[...], b_ref[...])


def _moe(h, gates_g, wg, wu, wd, ln_g, ln_b, alpha, tm):
    n, d = h.shape
    gw = wg.shape[1] // N_EXPERT_GROUPS
    d_expert = gw // EXPERTS_PER_GROUP
    full = lambda a: pl.BlockSpec(a.shape, lambda i, j: (0,) * a.ndim)
    return pl.pallas_call(
        functools.partial(_moe_kernel, alpha, d_expert),
        grid=(n // tm, N_EXPERT_GROUPS),
        in_specs=[pl.BlockSpec((tm, d), lambda i, j: (i, 0)),
                  pl.BlockSpec((1, tm, EXPERTS_PER_GROUP), lambda i, j: (j, i, 0)),
                  pl.BlockSpec((d, gw), lambda i, j: (0, j)),
                  pl.BlockSpec((d, gw), lambda i, j: (0, j)),
                  pl.BlockSpec((gw, d), lambda i, j: (j, 0)),
                  full(ln_g), full(ln_b)],
        out_specs=pl.BlockSpec((tm, d), lambda i, j: (i, 0)),
        out_shape=jax.ShapeDtypeStruct((n, d), F32),
        scratch_shapes=[pltpu.VMEM((tm, d), BF16), pltpu.VMEM((tm, d), F32)],
        compiler_params=_cparams(2),
        name="moe_ln",
    )(h, gates_g, wg, wu, wd, ln_g, ln_b)


def _post_block(x, att, ys, yq, lw, alpha, tm):
    n = x.shape[0]
    h, gates_t = _outproj(x, att, ys, yq, lw["wo"], lw["ln1_g"], lw["ln1_b"], lw["wr_t"], lw["br"], alpha, tm)
    gates_g = gates_t.reshape(N_EXPERT_GROUPS, EXPERTS_PER_GROUP, n).transpose(0, 2, 1)
    return _moe(h, gates_g, lw["wg"], lw["wu"], lw["wd"], lw["ln2_g"], lw["ln2_b"], alpha, tm)


def kernel(x_prompt, x_sample, cache_k, cache_v, cache_logf, page_table, state_ssm_re, state_ssm_im, state_pool, meta_tokens, ln0_g, ln0_b, w_in, b_forget, ssm_lam_re, ssm_lam_im, ssm_log_dt, ssm_b_re, ssm_b_im, ssm_c_re, ssm_c_im, ssm_d, w_glu, b_glu, w_pool, s_pool, w_out, ln1_g, ln1_b, w_router, b_router, w_gate, w_up, w_down, ln2_g, ln2_b):
    bp, seq, d_model = x_prompt.shape
    bs, t_new, _ = x_sample.shape
    depth = w_in.shape[0]
    n_pages, page = page_table.shape[1], cache_k.shape[2]
    past_len = n_pages * page
    length = N_META + seq
    lp = _round_up(length, LANES)
    tile = _pick_tile(lp, (640, 512, 384, 256, 128))
    alpha = (2 * depth) ** 0.25
    n_s = bs * t_new
    tm_s = _pick_tile(n_s, (512, 256, 128, 64, 32, 16, 8))
    assert t_new == SUBLANES and page == LANES and past_len >= POOL_BUF
    assert bs % DECODE_SEQS_PER_STEP == 0

    o1, o2, o3 = D_ATT, 2 * D_ATT, 3 * D_ATT
    o4 = o3 + N_HEADS
    row2 = lambda v: v.reshape(1, -1)

    kt_pages = cache_k.transpose(0, 1, 3, 4, 2).reshape(-1, depth, D_ATT, page)
    vt_pages = cache_v.transpose(0, 1, 3, 4, 2).reshape(-1, depth, D_ATT, page)
    lf_pages = cache_logf.transpose(0, 1, 3, 2)

    meta = jnp.broadcast_to(meta_tokens[None], (bp, N_META, d_model))
    xp = jnp.concatenate([meta, x_prompt, jnp.zeros((bp, lp - length, d_model), F32)], axis=1)
    xs = x_sample.reshape(n_s, d_model)
    wr_t = w_router.T
    br = b_router.reshape(N_EXPERTS, 1)
    nb_s = _pick_tile(bs, (16, 8, 4, 2, 1))

    outs_p = {k: [] for k in ("k", "v", "lf", "sr", "si", "pb")}
    outs_s = {k: [] for k in ("k", "v", "lf", "sr", "si", "pb")}
    for l in range(depth):
        w = w_in[l]
        wb = w.astype(BF16)
        wq = wb[:, :o1]
        wup = wb[:, o4:]
        wt = jnp.concatenate([wb[:, o1:o3], wb[:, o3:o4]], axis=1).T
        w_nat = jnp.concatenate([wb[:, :o3], wb[:, o4:]], axis=1)
        wf_t = wb[:, o3:o4].T
        bf = b_forget[l].reshape(N_HEADS, 1)

        apow_re, apow_im, bbr_t, bbi_t = _s5_params(ssm_lam_re[l], ssm_lam_im[l], ssm_log_dt[l],
                                                    ssm_b_re[l], ssm_b_im[l])
        apr = apow_re.reshape(SUBLANES, D_STATE)
        api = apow_im.reshape(SUBLANES, D_STATE)
        ridx = jnp.arange(SUBLANES)[:, None]
        ak = []
        for k in (1, 2, 4):
            ak.append(jnp.where(ridx >= k, apr[k - 1][None], 0.0))
            ak.append(jnp.where(ridx >= k, api[k - 1][None], 0.0))
        ak += [apr, api]
        mix_w = dict(
            bbr=_block_diag(bbr_t).astype(BF16), bbi=_block_diag(bbi_t).astype(BF16),
            cr=_block_diag(ssm_c_re[l].transpose(0, 2, 1)).astype(BF16),
            ci=_block_diag(ssm_c_im[l].transpose(0, 2, 1)).astype(BF16),
            d=row2(ssm_d[l]), ak=jnp.stack(ak), wglu=w_glu[l].astype(BF16), bglu=row2(b_glu[l]),
            wpool=_block_diag(w_pool[l]).astype(BF16), spool=row2(s_pool[l]))
        d_expert = w_gate.shape[-1]
        lw = dict(
            wo=w_out[l].astype(BF16), ln1_g=row2(ln1_g[l]), ln1_b=row2(ln1_b[l]), wr_t=wr_t, br=br,
            wg=w_gate[l].transpose(1, 0, 2).reshape(d_model, N_EXPERTS * d_expert).astype(BF16),
            wu=w_up[l].transpose(1, 0, 2).reshape(d_model, N_EXPERTS * d_expert).astype(BF16),
            wd=w_down[l].reshape(N_EXPERTS * d_expert, d_model).astype(BF16),
            ln2_g=row2(ln2_g[l]), ln2_b=row2(ln2_b[l]))

        res = _inproj_prompt(xp, row2(ln0_g), row2(ln0_b), wq, wup, wt, bf, l == 0, tile)
        if l == 0:
            xp, *res = res
        q, kt, vt, ka, vb, lft, u, p = res
        att = _attn_prompt(q, ka, vb, tile)
        ys, yq, sr, si = _seqmix(u, p, mix_w, length, tile)
        xp = _post_block(xp.reshape(bp * lp, d_model), att.reshape(bp * lp, D_ATT),
                         ys.reshape(bp * lp, D_SSM), yq.reshape(bp * lp, D_POOL), lw, alpha,
                         tile).reshape(bp, lp, d_model)
        outs_p["k"].append(kt[:, :, :length])
        outs_p["v"].append(vt[:, :, :length])
        outs_p["lf"].append(lft[:, :, :length])
        outs_p["sr"].append(sr.reshape(bp, N_SSM_GROUPS, SSM_STATE))
        outs_p["si"].append(si.reshape(bp, N_SSM_GROUPS, SSM_STATE))
        outs_p["pb"].append(p[:, length - POOL_BUF:length])

        res = _inproj_sample(xs, row2(ln0_g), row2(ln0_b), w_nat, wf_t, bf, l == 0, tm_s)
        if l == 0:
            xs, *res = res
        qkv, up, lfs_t = res
        lf_new = lfs_t.reshape(N_HEADS, bs, t_new).transpose(1, 0, 2)
        lf_new_pad = jnp.pad(lf_new, ((0, 0), (0, 0), (0, LANES - t_new)))
        att = _attn_decode(page_table, qkv.reshape(bs, t_new, 3 * D_ATT), lf_new_pad,
                           kt_pages, vt_pages, lf_pages, l)
        hist = jnp.concatenate([jnp.zeros((bs, 1, D_POOL), F32), state_pool[:, l]], axis=1)
        up3 = up.reshape(bs, t_new, D_SSM + D_POOL)
        p_s = up3[..., D_SSM:]
        ys, yq, sr, si = _seqmix_sample(up3, state_ssm_re[:, l].reshape(bs, 1, D_STATE),
                                        state_ssm_im[:, l].reshape(bs, 1, D_STATE), hist, mix_w, past_len,
                                        nb_s)
        xs = _post_block(xs, att.reshape(n_s, D_ATT), ys.reshape(n_s, D_SSM), yq.reshape(n_s, D_POOL),
                         lw, alpha, tm_s)
        qkv3 = qkv.reshape(bs, t_new, 3, N_HEADS, HEAD_DIM)
        outs_s["k"].append(qkv3[:, :, 1])
        outs_s["v"].append(qkv3[:, :, 2])
        outs_s["lf"].append(lf_new.transpose(0, 2, 1))
        outs_s["sr"].append(sr.reshape(bs, N_SSM_GROUPS, SSM_STATE))
        outs_s["si"].append(si.reshape(bs, N_SSM_GROUPS, SSM_STATE))
        outs_s["pb"].append(jnp.concatenate([state_pool[:, l], p_s], axis=1)[:, -POOL_BUF:])

    def heads_last(xs_t):
        a = jnp.stack(xs_t, axis=1)
        b_, dep, _, ln = a.shape
        return a.reshape(b_, dep, N_HEADS, HEAD_DIM, ln).transpose(0, 1, 4, 2, 3)

    y_prompt = xp[:, N_META:length]
    y_sample = xs.reshape(bs, t_new, d_model)
    return (y_prompt, y_sample,
            heads_last(outs_p["k"]), heads_last(outs_p["v"]),
            jnp.stack(outs_p["lf"], axis=1).transpose(0, 1, 3, 2),
            jnp.stack(outs_p["sr"], axis=1), jnp.stack(outs_p["si"], axis=1), jnp.stack(outs_p["pb"], axis=1),
            jnp.stack(outs_s["k"], axis=1), jnp.stack(outs_s["v"], axis=1), jnp.stack(outs_s["lf"], axis=1),
            jnp.stack(outs_s["sr"], axis=1), jnp.stack(outs_s["si"], axis=1), jnp.stack(outs_s["pb"], axis=1))
```

```python
import functools
import math

import jax
import jax.numpy as jnp
from jax import lax
from jax.experimental import pallas as pl
from jax.experimental.pallas import tpu as pltpu

F32 = jnp.float32
BF16 = jnp.bfloat16

N_META = 16
N_HEADS = 8
HEAD_DIM = 64
D_ATT = N_HEADS * HEAD_DIM
SSM_GROUP = 16
N_SSM_GROUPS = 16
SSM_STATE = 64
D_SSM = SSM_GROUP * N_SSM_GROUPS
D_STATE = N_SSM_GROUPS * SSM_STATE
POOL_WINDOWS = (2, 4, 8, 16)
POOL_GROUP = 64
D_POOL = POOL_GROUP * len(POOL_WINDOWS)
POOL_BUF = max(POOL_WINDOWS) - 1
POOL_HIST = POOL_BUF + 1
N_EXPERTS = 16
N_EXPERT_GROUPS = 4
EXPERTS_PER_GROUP = N_EXPERTS // N_EXPERT_GROUPS
TOP_K = 2
SCALE = HEAD_DIM ** -0.5
LOG2E = math.log2(math.e)
LN_EPS = 1e-5
NEG_INF = -1e30
HEADS_PER_STEP = 2
BIAS_PIECES = 3
SOFTMAX_ROWS = 32
DECODE_SEQS_PER_STEP = 2

LANES = 128
SUBLANES = 8
VMEM_LIMIT = 56 * 1024 * 1024


def _cparams(n_axes):
    return pltpu.CompilerParams(
        dimension_semantics=("arbitrary",) * n_axes, vmem_limit_bytes=VMEM_LIMIT)


def _round_up(x, m):
    return (x + m - 1) // m * m


def _pick_tile(n, candidates):
    for c in candidates:
        if n % c == 0:
            return c
    raise ValueError(f"no tile for {n}")


def _layer_norm(x, g, b):
    mu = jnp.mean(x, axis=-1, keepdims=True)
    xc = x - mu
    var = jnp.mean(xc * xc, axis=-1, keepdims=True)
    return xc * lax.rsqrt(var + LN_EPS) * g + b


def _log_sigmoid(x):
    return -(jnp.maximum(-x, 0.0) + jnp.log1p(jnp.exp(-jnp.abs(x))))


def _dot(a, b):
    return jnp.dot(a, b, preferred_element_type=F32)


def _dot_nt(a, b, precision=None):
    return lax.dot_general(a, b, (((1,), (1,)), ((), ())),
                           preferred_element_type=F32, precision=precision)


def _lane_cumsum(x):
    lane = lax.broadcasted_iota(jnp.int32, x.shape, 1)
    s = 1
    while s < LANES:
        x = x + jnp.where(lane >= s, pltpu.roll(x, s, axis=1), 0.0)
        s *= 2
    return x


def _s5_param_kernel(lr_ref, li_ref, ldt_ref, br_ref, bi_ref, apow_re, apow_im, bbr_ref, bbi_ref):
    lr = lr_ref[...]
    li = li_ref[...]
    dt = jnp.exp(ldt_ref[...])
    mag = jnp.exp(lr * dt)
    ab_re = mag * jnp.cos(li * dt)
    ab_im = mag * jnp.sin(li * dt)
    nr, ni = ab_re - 1.0, ab_im
    den = lr * lr + li * li
    gr = (nr * lr + ni * li) / den
    gi = (ni * lr - nr * li) / den
    br = br_ref[...]
    bi = bi_ref[...]
    bbr_ref[...] = gr[:, None, :] * br - gi[:, None, :] * bi
    bbi_ref[...] = gr[:, None, :] * bi + gi[:, None, :] * br
    pr, pi = ab_re, ab_im
    apow_re[0] = pr
    apow_im[0] = pi
    for k in range(1, SUBLANES):
        pr, pi = pr * ab_re - pi * ab_im, pr * ab_im + pi * ab_re
        apow_re[k] = pr
        apow_im[k] = pi


def _s5_params(lam_re, lam_im, log_dt, b_re, b_im):
    g, p = lam_re.shape
    c = b_re.shape[-1]
    out_shape = (jax.ShapeDtypeStruct((SUBLANES, g, p), F32), jax.ShapeDtypeStruct((SUBLANES, g, p), F32),
                 jax.ShapeDtypeStruct((g, c, p), F32), jax.ShapeDtypeStruct((g, c, p), F32))
    return pl.pallas_call(_s5_param_kernel, out_shape=out_shape, name="s5_params")(
        lam_re, lam_im, log_dt.reshape(g, 1), b_re.transpose(0, 2, 1), b_im.transpose(0, 2, 1))


def _block_diag(w):
    g, a, b = w.shape
    eye = jnp.eye(g, dtype=w.dtype)
    return (w[:, :, None, :] * eye[:, None, :, None]).reshape(g * a, g * b)


def _inproj_prompt_kernel(apply_ln, tm, x_ref, g_ref, b_ref, wq_ref, wup_ref, wt_ref, bf_ref, *outs):
    if apply_ln:
        xn_ref, q_ref, kt_ref, vt_ref, ka_ref, vb_ref, lft_ref, u_ref, p_ref, carry_ref, ck_scr = outs
    else:
        q_ref, kt_ref, vt_ref, ka_ref, vb_ref, lft_ref, u_ref, p_ref, carry_ref, ck_scr = outs
    t = pl.program_id(1)
    x = x_ref[0]
    if apply_ln:
        x = _layer_norm(x, g_ref[...], b_ref[...])
        xn_ref[0] = x
    xb = x.astype(BF16)
    q = _dot(xb, wq_ref[...]) * (SCALE * LOG2E)
    lane = lax.broadcasted_iota(jnp.int32, (tm, LANES), 1)
    ones = jnp.where(lane < HEAD_DIM + BIAS_PIECES, 1.0, 0.0)
    for h in range(N_HEADS):
        pair = q[:, (h // 2) * LANES:(h // 2 + 1) * LANES]
        if h % 2:
            pair = pltpu.roll(pair, HEAD_DIM, axis=1)
        q_ref[0, h] = jnp.where(lane < HEAD_DIM, pair, ones).astype(BF16)
    up = _dot(xb, wup_ref[...])
    u_ref[0] = up[:, :D_SSM]
    p_ref[0] = up[:, D_SSM:]
    zt = _dot_nt(wt_ref[...], xb)
    kt = zt[:D_ATT]
    vt = zt[D_ATT:2 * D_ATT]
    kt_ref[0] = kt
    vt_ref[0] = vt
    ones_row = lax.broadcasted_iota(jnp.int32, (HEAD_DIM, tm), 0) == 0
    for h in range(N_HEADS):
        vb_ref[0, h, 0:HEAD_DIM, :] = vt[h * HEAD_DIM:(h + 1) * HEAD_DIM].astype(BF16)
        vb_ref[0, h, HEAD_DIM:, :] = jnp.where(ones_row, 1.0, 0.0).astype(BF16)
    lf = _log_sigmoid(zt[2 * D_ATT:] + bf_ref[...])
    lft_ref[0] = lf

    @pl.when(t == 0)
    def _():
        carry_ref[...] = jnp.zeros_like(carry_ref)

    carry = carry_ref[...]
    for c in range(tm // LANES):
        blk = _lane_cumsum(lf[:, c * LANES:(c + 1) * LANES]) + carry
        ck_scr[:, c * LANES:(c + 1) * LANES] = blk
        carry = jnp.broadcast_to(blk[:, LANES - 1:LANES], carry.shape)
    carry_ref[...] = carry

    piece_row = lax.broadcasted_iota(jnp.int32, (HEAD_DIM, tm), 0)
    for h in range(N_HEADS):
        ka_ref[0, h, 0:HEAD_DIM, :] = kt[h * HEAD_DIM:(h + 1) * HEAD_DIM].astype(BF16)
        rest = ck_scr[h:h + 1, :] * (-LOG2E)
        aug = jnp.zeros((HEAD_DIM, tm), F32)
        for i in range(BIAS_PIECES):
            piece = rest.astype(BF16).astype(F32)
            aug = jnp.where(piece_row == i, piece, aug)
            rest = rest - piece
        ka_ref[0, h, HEAD_DIM:, :] = aug.astype(BF16)


def _inproj_prompt(x, ln_g, ln_b, wq, wup, wt, bf, apply_ln, tm):
    b, lp, d = x.shape
    nt = lp // tm
    full = lambda shape: pl.BlockSpec(shape, lambda i, j: (0,) * len(shape))
    out_shape, out_specs = [], []
    if apply_ln:
        out_shape.append(jax.ShapeDtypeStruct((b, lp, d), F32))
        out_specs.append(pl.BlockSpec((1, tm, d), lambda i, j: (i, j, 0)))
    out_shape += [
        jax.ShapeDtypeStruct((b, N_HEADS, lp, LANES), BF16),
        jax.ShapeDtypeStruct((b, D_ATT, lp), F32),
        jax.ShapeDtypeStruct((b, D_ATT, lp), F32),
        jax.ShapeDtypeStruct((b, N_HEADS, LANES, lp), BF16),
        jax.ShapeDtypeStruct((b, N_HEADS, LANES, lp), BF16),
        jax.ShapeDtypeStruct((b, N_HEADS, lp), F32),
        jax.ShapeDtypeStruct((b, lp, D_SSM), F32),
        jax.ShapeDtypeStruct((b, lp, D_POOL), F32),
    ]
    out_specs += [
        pl.BlockSpec((1, N_HEADS, tm, LANES), lambda i, j: (i, 0, j, 0)),
        pl.BlockSpec((1, D_ATT, tm), lambda i, j: (i, 0, j)),
        pl.BlockSpec((1, D_ATT, tm), lambda i, j: (i, 0, j)),
        pl.BlockSpec((1, N_HEADS, LANES, tm), lambda i, j: (i, 0, 0, j)),
        pl.BlockSpec((1, N_HEADS, LANES, tm), lambda i, j: (i, 0, 0, j)),
        pl.BlockSpec((1, N_HEADS, tm), lambda i, j: (i, 0, j)),
        pl.BlockSpec((1, tm, D_SSM), lambda i, j: (i, j, 0)),
        pl.BlockSpec((1, tm, D_POOL), lambda i, j: (i, j, 0)),
    ]
    return pl.pallas_call(
        functools.partial(_inproj_prompt_kernel, apply_ln, tm),
        grid=(b, nt),
        in_specs=[pl.BlockSpec((1, tm, d), lambda i, j: (i, j, 0)),
                  full(ln_g.shape), full(ln_b.shape), full(wq.shape), full(wup.shape),
                  full(wt.shape), full(bf.shape)],
        out_specs=out_specs,
        out_shape=out_shape,
        scratch_shapes=[pltpu.VMEM((N_HEADS, LANES), F32), pltpu.VMEM((N_HEADS, tm), F32)],
        compiler_params=_cparams(2),
        name="inproj_prompt",
    )(x, ln_g, ln_b, wq, wup, wt, bf)


def _inproj_sample_kernel(apply_ln, x_ref, g_ref, b_ref, w_ref, wf_ref, bf_ref, *outs):
    if apply_ln:
        xn_ref, qkv_ref, up_ref, lft_ref = outs
    else:
        qkv_ref, up_ref, lft_ref = outs
    x = x_ref[...]
    if apply_ln:
        x = _layer_norm(x, g_ref[...], b_ref[...])
        xn_ref[...] = x
    xb = x.astype(BF16)
    z = _dot(xb, w_ref[...])
    qkv_ref[...] = z[:, :3 * D_ATT]
    up_ref[...] = z[:, 3 * D_ATT:]
    lft_ref[...] = _log_sigmoid(_dot_nt(wf_ref[...], xb) + bf_ref[...])


def _inproj_sample(x, ln_g, ln_b, w, wf, bf, apply_ln, tm):
    n, d = x.shape
    full = lambda shape: pl.BlockSpec(shape, lambda i: (0,) * len(shape))
    out_shape, out_specs = [], []
    if apply_ln:
        out_shape.append(jax.ShapeDtypeStruct((n, d), F32))
        out_specs.append(pl.BlockSpec((tm, d), lambda i: (i, 0)))
    out_shape += [jax.ShapeDtypeStruct((n, 3 * D_ATT), F32),
                  jax.ShapeDtypeStruct((n, D_SSM + D_POOL), F32),
                  jax.ShapeDtypeStruct((N_HEADS, n), F32)]
    out_specs += [pl.BlockSpec((tm, 3 * D_ATT), lambda i: (i, 0)),
                  pl.BlockSpec((tm, D_SSM + D_POOL), lambda i: (i, 0)),
                  pl.BlockSpec((N_HEADS, tm), lambda i: (0, i))]
    return pl.pallas_call(
        functools.partial(_inproj_sample_kernel, apply_ln),
        grid=(n // tm,),
        in_specs=[pl.BlockSpec((tm, d), lambda i: (i, 0)),
                  full(ln_g.shape), full(ln_b.shape), full(w.shape), full(wf.shape), full(bf.shape)],
        out_specs=out_specs,
        out_shape=out_shape,
        compiler_params=_cparams(1),
        name="inproj_sample",
    )(x, ln_g, ln_b, w, wf, bf)


def _attn_prompt_kernel(tq, q_ref, ka_ref, vb_ref, o_ref,
                        s_a, s_b, p_a, p_b, al_a, al_b, m_scr, acc_scr):
    qi = pl.program_id(2)
    rc = SOFTMAX_ROWS
    rep = tq // LANES
    m_scr[...] = jnp.full(m_scr.shape, NEG_INF, F32)
    acc_scr[...] = jnp.zeros(acc_scr.shape, F32)
    row = lax.broadcasted_iota(jnp.int32, (rc, tq), 0)
    col = lax.broadcasted_iota(jnp.int32, (rc, tq), 1)

    def scores(k, s_buf, j):
        k0 = pl.multiple_of(k * tq, LANES)
        s_buf[j] = _dot(q_ref[0, j], ka_ref[0, j, :, pl.ds(k0, tq)])

    def chunk(s_buf, j, c, masked):
        rows = slice(c * rc, (c + 1) * rc)
        sc = s_buf[j, rows, :]
        if masked:
            sc = jnp.where(col <= row + c * rc, sc, NEG_INF)
        return rows, sc

    def row_max(s_buf, al_buf, j, masked):
        for c in range(tq // rc):
            rows, sc = chunk(s_buf, j, c, masked)
            m_old = m_scr[j, rows, :]
            m_new = jnp.maximum(m_old, jnp.max(sc, axis=1, keepdims=True))
            al_buf[j, rows, :] = jnp.exp2(m_old - m_new)
            m_scr[j, rows, :] = m_new

    def probs(s_buf, p_buf, j, masked):
        for c in range(tq // rc):
            rows, sc = chunk(s_buf, j, c, masked)
            x = sc - pltpu.repeat(m_scr[j, rows, :], rep, axis=1)
            p_buf[j, rows, :] = jnp.exp2(x.astype(BF16))

    def values(k, p_buf, al_buf, j):
        k0 = pl.multiple_of(k * tq, LANES)
        acc_scr[j] = al_buf[j] * acc_scr[j] + _dot_nt(p_buf[j], vb_ref[0, j, :, pl.ds(k0, tq)])

    for j in range(HEADS_PER_STEP):
        scores(0, s_a, j)
    p_b[...] = jnp.zeros(p_b.shape, BF16)
    al_b[...] = jnp.ones(al_b.shape, F32)

    def stage(k, s_cur, s_nxt, p_cur, p_prev, al_cur, al_prev):
        k_prev = jnp.maximum(k - 1, 0)
        for j in range(HEADS_PER_STEP):
            scores(k + 1, s_nxt, j)
        for j in range(HEADS_PER_STEP):
            row_max(s_cur, al_cur, j, False)
        for j in range(HEADS_PER_STEP):
            probs(s_cur, p_cur, j, False)
        for j in range(HEADS_PER_STEP):
            values(k_prev, p_prev, al_prev, j)

    def body(k, carry):
        @pl.when(k % 2 == 0)
        def _():
            stage(k, s_a, s_b, p_a, p_b, al_a, al_b)

        @pl.when(k % 2 == 1)
        def _():
            stage(k, s_b, s_a, p_b, p_a, al_b, al_a)

        return carry

    lax.fori_loop(0, qi, body, 0)

    def drain(s_cur, p_cur, p_prev, al_cur, al_prev):
        k_prev = jnp.maximum(qi - 1, 0)
        for j in range(HEADS_PER_STEP):
            row_max(s_cur, al_cur, j, True)
        for j in range(HEADS_PER_STEP):
            values(k_prev, p_prev, al_prev, j)
            probs(s_cur, p_cur, j, True)
        for j in range(HEADS_PER_STEP):
            values(qi, p_cur, al_cur, j)

    @pl.when(qi % 2 == 0)
    def _():
        drain(s_a, p_a, p_b, al_a, al_b)

    @pl.when(qi % 2 == 1)
    def _():
        drain(s_b, p_b, p_a, al_b, al_a)

    outs = []
    for j in range(HEADS_PER_STEP):
        acc = acc_scr[j]
        outs.append(acc[:, 0:HEAD_DIM] / acc[:, HEAD_DIM:HEAD_DIM + 1])
    o_ref[0] = jnp.concatenate(outs, axis=1).astype(o_ref.dtype)


def _attn_prompt(q, ka, vb, tq):
    b, _, lp, _ = q.shape
    hp = N_HEADS // HEADS_PER_STEP
    rows = HEADS_PER_STEP * HEAD_DIM
    return pl.pallas_call(
        functools.partial(_attn_prompt_kernel, tq),
        grid=(b, hp, lp // tq),
        in_specs=[pl.BlockSpec((1, HEADS_PER_STEP, tq, LANES), lambda i, h, j: (i, h, j, 0)),
                  pl.BlockSpec((1, HEADS_PER_STEP, LANES, lp), lambda i, h, j: (i, h, 0, 0)),
                  pl.BlockSpec((1, HEADS_PER_STEP, LANES, lp), lambda i, h, j: (i, h, 0, 0))],
        out_specs=pl.BlockSpec((1, tq, rows), lambda i, h, j: (i, j, h)),
        out_shape=jax.ShapeDtypeStruct((b, lp, D_ATT), BF16),
        scratch_shapes=[pltpu.VMEM((HEADS_PER_STEP, tq, tq), F32), pltpu.VMEM((HEADS_PER_STEP, tq, tq), F32),
                        pltpu.VMEM((HEADS_PER_STEP, tq, tq), BF16), pltpu.VMEM((HEADS_PER_STEP, tq, tq), BF16),
                        pltpu.VMEM((HEADS_PER_STEP, tq, LANES), F32), pltpu.VMEM((HEADS_PER_STEP, tq, LANES), F32),
                        pltpu.VMEM((HEADS_PER_STEP, tq, LANES), F32),
                        pltpu.VMEM((HEADS_PER_STEP, tq, LANES), F32)],
        compiler_params=_cparams(3),
        name="attn_prompt",
    )(q, ka, vb)


def _attn_decode_kernel(n_pages, t_new, pt_ref, qkv_ref, lfn_ref, *refs):
    o_ref = refs[-1]
    rows = t_new * N_HEADS
    head_of_row = lax.broadcasted_iota(jnp.int32, (N_HEADS, D_ATT), 0)
    head_of_col = lax.broadcasted_iota(jnp.int32, (N_HEADS, D_ATT), 1) // HEAD_DIM
    hmask = head_of_row == head_of_col
    omask = jnp.concatenate([hmask] * t_new, axis=0)
    key_t = lax.broadcasted_iota(jnp.int32, (rows, LANES), 1)
    qry_t = lax.broadcasted_iota(jnp.int32, (rows, LANES), 0) // N_HEADS
    pad = jnp.zeros((LANES - t_new, D_ATT), F32)

    def expand(c):
        return jnp.concatenate([c] * t_new, axis=0)

    for sq in range(DECODE_SEQS_PER_STEP):
        base = 3 * n_pages * sq
        k_refs = refs[base:base + n_pages]
        v_refs = refs[base + n_pages:base + 2 * n_pages]
        lf_refs = refs[base + 2 * n_pages:base + 3 * n_pages]
        qkv = qkv_ref[sq]
        q = qkv[:, :D_ATT] * SCALE
        k_new = qkv[:, D_ATT:2 * D_ATT]
        v_new = qkv[:, 2 * D_ATT:]
        qbd = jnp.concatenate(
            [jnp.where(hmask, jnp.broadcast_to(q[t:t + 1], (N_HEADS, D_ATT)), 0.0) for t in range(t_new)],
            axis=0).astype(BF16)

        local = _lane_cumsum(jnp.concatenate([lf_refs[j][0, 0] for j in range(n_pages)] + [lfn_ref[sq]],
                                             axis=0))
        off = jnp.zeros((N_HEADS, LANES), F32)
        s_blocks = []
        for j in range(n_pages):
            ck = local[j * N_HEADS:(j + 1) * N_HEADS] + off
            off = jnp.broadcast_to(ck[:, LANES - 1:LANES], off.shape)
            s = _dot(qbd, k_refs[j][0, 0].astype(BF16))
            s_blocks.append(s - expand(ck))
        k_pad = jnp.concatenate([k_new, pad], axis=0).astype(BF16)
        v_pad = jnp.concatenate([v_new, pad], axis=0).astype(BF16)
        ck_new = local[n_pages * N_HEADS:] + off
        s_new = _dot_nt(qbd, k_pad) - expand(ck_new)
        s_blocks.append(jnp.where(key_t <= qry_t, s_new, NEG_INF))

        m = s_blocks[0]
        for s in s_blocks[1:]:
            m = jnp.maximum(m, s)
        m = m.max(axis=1, keepdims=True)
        l = jnp.zeros((rows, LANES), F32)
        acc = jnp.zeros((rows, D_ATT), F32)
        for j, s in enumerate(s_blocks):
            p = jnp.exp(s - m)
            l = l + p
            if j < n_pages:
                acc = acc + _dot_nt(p.astype(BF16), v_refs[j][0, 0].astype(BF16))
            else:
                acc = acc + _dot(p.astype(BF16), v_pad)
        acc = acc / l.sum(axis=1, keepdims=True)
        acc = jnp.where(omask, acc, 0.0)
        o_ref[sq] = acc.reshape(t_new, N_HEADS, D_ATT).sum(axis=1).astype(o_ref.dtype)


def _attn_decode(page_table, qkv, lf_new_t, kt_pages, vt_pages, lf_pages, layer):
    bs, t_new, _ = qkv.shape
    n_pages = page_table.shape[1]
    page = kt_pages.shape[-1]
    sps = DECODE_SEQS_PER_STEP
    pt_flat = page_table.reshape(-1)

    def page_spec(rows, sq, j):
        return pl.BlockSpec((1, 1, rows, page), lambda i, pt: (pt[(i * sps + sq) * n_pages + j], layer, 0, 0))

    in_specs = [pl.BlockSpec((sps, t_new, 3 * D_ATT), lambda i, pt: (i, 0, 0)),
                pl.BlockSpec((sps, N_HEADS, LANES), lambda i, pt: (i, 0, 0))]
    operands = []
    for sq in range(sps):
        in_specs += [page_spec(D_ATT, sq, j) for j in range(n_pages)]
        in_specs += [page_spec(D_ATT, sq, j) for j in range(n_pages)]
        in_specs += [page_spec(N_HEADS, sq, j) for j in range(n_pages)]
        operands += [kt_pages] * n_pages + [vt_pages] * n_pages + [lf_pages] * n_pages
    grid_spec = pltpu.PrefetchScalarGridSpec(
        num_scalar_prefetch=1, grid=(bs // sps,), in_specs=in_specs,
        out_specs=pl.BlockSpec((sps, t_new, D_ATT), lambda i, pt: (i, 0, 0)))
    return pl.pallas_call(
        functools.partial(_attn_decode_kernel, n_pages, t_new),
        grid_spec=grid_spec,
        out_shape=jax.ShapeDtypeStruct((bs, t_new, D_ATT), BF16),
        compiler_params=_cparams(1),
        name="attn_decode",
    )(pt_flat, qkv, lf_new_t, *operands)


def _s5_scan_rows(br, bi, cr, ci, ak_ref):
    for idx, k in enumerate((1, 2, 4)):
        akr = ak_ref[2 * idx]
        aki = ak_ref[2 * idx + 1]
        sr = pltpu.roll(br, k, axis=0)
        si = pltpu.roll(bi, k, axis=0)
        br, bi = br + akr * sr - aki * si, bi + akr * si + aki * sr
    pr = ak_ref[6]
    pi = ak_ref[7]
    return br + pr * cr - pi * ci, bi + pr * ci + pi * cr


def _s5_readout(xr, xi, u, cr_ref, ci_ref, d_ref, wglu_ref, bglu_ref):
    y = _dot(xr.astype(BF16), cr_ref[...]) - _dot(xi.astype(BF16), ci_ref[...]) + d_ref[...] * u
    y = jax.nn.gelu(y)
    return y * jax.nn.sigmoid(_dot(y.astype(BF16), wglu_ref[...]) + bglu_ref[...])


def _pool_window_sums(e):
    c2 = e + pltpu.roll(e, 1, axis=0)
    c4 = c2 + pltpu.roll(c2, 2, axis=0)
    c8 = c4 + pltpu.roll(c4, 4, axis=0)
    c16 = c8 + pltpu.roll(c8, 8, axis=0)
    return c2, c4, c8, c16


def _pool_mix(sums, tok, pos, wpool_ref, spool_ref):
    c2, c4, c8, c16 = sums
    grp = lax.broadcasted_iota(jnp.int32, tok.shape, 1) // POOL_GROUP
    win = jnp.where(grp == 0, c2, jnp.where(grp == 1, c4, jnp.where(grp == 2, c8, c16)))
    wsize = jnp.where(grp == 0, 2, jnp.where(grp == 1, 4, jnp.where(grp == 2, 8, 16)))
    cnt = jnp.minimum(wsize, pos + 1).astype(F32)
    mm = win / cnt - tok
    return _dot(mm.astype(BF16), wpool_ref[...]) * spool_ref[...]


def _seqmix_kernel(tt, snap_tile, snap_row,
                   u_ref, p_ref, bbr_ref, bbi_ref, cr_ref, ci_ref, d_ref, ak_ref, wglu_ref, bglu_ref,
                   wpool_ref, spool_ref,
                   ys_ref, yq_ref, sr_ref, si_ref,
                   xr_scr, xi_scr, car_r, car_i, ext_scr):
    t = pl.program_id(1)

    @pl.when(t == 0)
    def _():
        car_r[...] = jnp.zeros_like(car_r)
        car_i[...] = jnp.zeros_like(car_i)
        ext_scr[0:POOL_HIST, :] = jnp.zeros((POOL_HIST, D_POOL), F32)

    u = u_ref[0]
    ub = u.astype(BF16)
    xr_scr[...] = _dot(ub, bbr_ref[...])
    xi_scr[...] = _dot(ub, bbi_ref[...])

    def scan_block(i, carry):
        cr, ci = carry
        r0 = pl.multiple_of(i * SUBLANES, SUBLANES)
        xr, xi = _s5_scan_rows(xr_scr[pl.ds(r0, SUBLANES), :], xi_scr[pl.ds(r0, SUBLANES), :], cr, ci, ak_ref)
        xr_scr[pl.ds(r0, SUBLANES), :] = xr
        xi_scr[pl.ds(r0, SUBLANES), :] = xi
        return xr[SUBLANES - 1:SUBLANES], xi[SUBLANES - 1:SUBLANES]

    cr, ci = lax.fori_loop(0, tt // SUBLANES, scan_block, (car_r[...], car_i[...]))
    car_r[...] = cr
    car_i[...] = ci

    @pl.when(t == snap_tile)
    def _():
        sr_ref[0] = xr_scr[snap_row:snap_row + 1, :]
        si_ref[0] = xi_scr[snap_row:snap_row + 1, :]

    ys_ref[0] = _s5_readout(xr_scr[...], xi_scr[...], u, cr_ref, ci_ref, d_ref, wglu_ref,
                            bglu_ref).astype(ys_ref.dtype)

    ext_scr[POOL_HIST:POOL_HIST + tt, :] = p_ref[0]
    e = ext_scr[...]
    sums = [c[POOL_HIST:] for c in _pool_window_sums(e)]
    pos = t * tt + lax.broadcasted_iota(jnp.int32, (tt, D_POOL), 0)
    yq_ref[0] = _pool_mix(sums, e[POOL_HIST:], pos, wpool_ref, spool_ref).astype(yq_ref.dtype)
    ext_scr[0:POOL_HIST, :] = e[tt:tt + POOL_HIST]


def _mix_weights(prm):
    return (prm["bbr"], prm["bbi"], prm["cr"], prm["ci"], prm["d"], prm["ak"], prm["wglu"],
            prm["bglu"], prm["wpool"], prm["spool"])


def _seqmix(u, p, prm, length, tt):
    b, lp, _ = u.shape
    nt = lp // tt
    snap_tile, snap_row = (length - 1) // tt, (length - 1) % tt
    full = lambda a: pl.BlockSpec(a.shape, lambda i, j: (0,) * a.ndim)
    weights = _mix_weights(prm)
    return pl.pallas_call(
        functools.partial(_seqmix_kernel, tt, snap_tile, snap_row),
        grid=(b, nt),
        in_specs=[pl.BlockSpec((1, tt, D_SSM), lambda i, j: (i, j, 0)),
                  pl.BlockSpec((1, tt, D_POOL), lambda i, j: (i, j, 0))]
                 + [full(w) for w in weights],
        out_specs=[pl.BlockSpec((1, tt, D_SSM), lambda i, j: (i, j, 0)),
                   pl.BlockSpec((1, tt, D_POOL), lambda i, j: (i, j, 0)),
                   pl.BlockSpec((1, 1, D_STATE), lambda i, j: (i, 0, 0)),
                   pl.BlockSpec((1, 1, D_STATE), lambda i, j: (i, 0, 0))],
        out_shape=[jax.ShapeDtypeStruct((b, lp, D_SSM), BF16),
                   jax.ShapeDtypeStruct((b, lp, D_POOL), BF16),
                   jax.ShapeDtypeStruct((b, 1, D_STATE), F32),
                   jax.ShapeDtypeStruct((b, 1, D_STATE), F32)],
        scratch_shapes=[pltpu.VMEM((tt, D_STATE), F32), pltpu.VMEM((tt, D_STATE), F32),
                        pltpu.VMEM((1, D_STATE), F32), pltpu.VMEM((1, D_STATE), F32),
                        pltpu.VMEM((POOL_HIST + tt, D_POOL), F32)],
        compiler_params=_cparams(2),
        name="seqmix",
    )(u, p, *weights)


def _seqmix_sample_kernel(nb, pos0, up_ref, x0r_ref, x0i_ref, hist_ref,
                          bbr_ref, bbi_ref, cr_ref, ci_ref, d_ref, ak_ref, wglu_ref, bglu_ref,
                          wpool_ref, spool_ref,
                          ys_ref, yq_ref, sr_ref, si_ref, xr_scr, xi_scr):
    t_new = SUBLANES
    up = up_ref[...].reshape(nb * t_new, D_SSM + D_POOL)
    u = up[:, :D_SSM]
    ub = u.astype(BF16)
    xr_scr[...] = _dot(ub, bbr_ref[...])
    xi_scr[...] = _dot(ub, bbi_ref[...])

    def scan_block(i, carry):
        r0 = pl.multiple_of(i * SUBLANES, SUBLANES)
        xr, xi = _s5_scan_rows(xr_scr[pl.ds(r0, SUBLANES), :], xi_scr[pl.ds(r0, SUBLANES), :],
                               x0r_ref[i], x0i_ref[i], ak_ref)
        xr_scr[pl.ds(r0, SUBLANES), :] = xr
        xi_scr[pl.ds(r0, SUBLANES), :] = xi
        sr_ref[i] = xr[SUBLANES - 1:SUBLANES]
        si_ref[i] = xi[SUBLANES - 1:SUBLANES]
        return carry

    lax.fori_loop(0, nb, scan_block, 0)
    ys = _s5_readout(xr_scr[...], xi_scr[...], u, cr_ref, ci_ref, d_ref, wglu_ref, bglu_ref)
    ys_ref[...] = ys.reshape(nb, t_new, D_SSM).astype(ys_ref.dtype)

    tok = up[:, D_SSM:].reshape(nb, t_new, D_POOL)
    per_seq = POOL_HIST + t_new
    e = jnp.concatenate([hist_ref[...], tok], axis=1).reshape(nb * per_seq, D_POOL)
    sums = [c.reshape(nb, per_seq, D_POOL)[:, POOL_HIST:].reshape(nb * t_new, D_POOL)
            for c in _pool_window_sums(e)]
    pos = pos0 + lax.broadcasted_iota(jnp.int32, (nb, t_new, D_POOL), 1).reshape(nb * t_new, D_POOL)
    yq = _pool_mix(sums, up[:, D_SSM:], pos, wpool_ref, spool_ref)
    yq_ref[...] = yq.reshape(nb, t_new, D_POOL).astype(yq_ref.dtype)


def _seqmix_sample(up, x0r, x0i, hist, prm, pos0, nb):
    bs, t_new, width = up.shape
    full = lambda a: pl.BlockSpec(a.shape, lambda i: (0,) * a.ndim)
    seq = lambda r, w: pl.BlockSpec((nb, r, w), lambda i: (i, 0, 0))
    weights = _mix_weights(prm)
    return pl.pallas_call(
        functools.partial(_seqmix_sample_kernel, nb, pos0),
        grid=(bs // nb,),
        in_specs=[seq(t_new, width), seq(1, D_STATE), seq(1, D_STATE), seq(POOL_HIST, D_POOL)]
                 + [full(w) for w in weights],
        out_specs=[seq(t_new, D_SSM), seq(t_new, D_POOL), seq(1, D_STATE), seq(1, D_STATE)],
        out_shape=[jax.ShapeDtypeStruct((bs, t_new, D_SSM), BF16),
                   jax.ShapeDtypeStruct((bs, t_new, D_POOL), BF16),
                   jax.ShapeDtypeStruct((bs, 1, D_STATE), F32),
                   jax.ShapeDtypeStruct((bs, 1, D_STATE), F32)],
        scratch_shapes=[pltpu.VMEM((nb * t_new, D_STATE), F32), pltpu.VMEM((nb * t_new, D_STATE), F32)],
        compiler_params=_cparams(1),
        name="seqmix_sample",
    )(up, x0r, x0i, hist, *weights)


def _outproj_kernel(alpha, x_ref, att_ref, ys_ref, yq_ref, wo_ref, g_ref, b_ref, wr_ref, br_ref,
                    h_ref, gates_ref):
    y = alpha * x_ref[...]
    y = y + _dot(att_ref[...], wo_ref[0:D_ATT, :])
    y = y + _dot(ys_ref[...], wo_ref[D_ATT:D_ATT + D_SSM, :])
    y = y + _dot(yq_ref[...], wo_ref[D_ATT + D_SSM:, :])
    h = _layer_norm(y, g_ref[...], b_ref[...])
    h_ref[...] = h
    h_hi = h.astype(BF16)
    h_lo = (h - h_hi.astype(F32)).astype(BF16)
    both = _dot_nt(wr_ref[...], h_hi)
    logits = both[:N_EXPERTS] + both[N_EXPERTS:] + _dot_nt(wr_ref[0:N_EXPERTS, :], h_lo)
    aff = jax.nn.sigmoid(logits)
    biased = aff + br_ref[...]
    rows = [biased[e:e + 1, :] for e in range(N_EXPERTS)]

    def beats(a, ia, c, ic):
        return (a >= c) if ia < ic else (a > c)

    gscore = []
    for g in range(N_EXPERT_GROUPS):
        members = list(range(g * EXPERTS_PER_GROUP, (g + 1) * EXPERTS_PER_GROUP))
        total = jnp.zeros_like(rows[0])
        for e in members:
            rank = jnp.zeros_like(rows[0])
            for o in members:
                if o != e:
                    rank = rank + beats(rows[o], o, rows[e], e).astype(F32)
            total = total + jnp.where(rank < 2.0, rows[e], 0.0)
        gscore.append(total)
    in_group = []
    for g in range(N_EXPERT_GROUPS):
        lost = jnp.zeros_like(rows[0])
        for o in range(N_EXPERT_GROUPS):
            if o != g:
                lost = lost + beats(gscore[o], o, gscore[g], g).astype(F32)
        in_group.append(lost < 1.0)
    masked = [jnp.where(in_group[e // EXPERTS_PER_GROUP], rows[e], NEG_INF) for e in range(N_EXPERTS)]
    sel = []
    for e in range(N_EXPERTS):
        rank = jnp.zeros_like(rows[0])
        for o in range(N_EXPERTS):
            if o != e:
                rank = rank + beats(masked[o], o, masked[e], e).astype(F32)
        sel.append(jnp.where(rank < float(TOP_K), aff[e:e + 1, :], 0.0))
    denom = sel[0]
    for e in range(1, N_EXPERTS):
        denom = denom + sel[e]
    gates_ref[...] = jnp.concatenate(sel, axis=0) / denom


def _outproj(x, att, ys, yq, wo, ln_g, ln_b, wr_t, br, alpha, tm):
    n, d = x.shape
    full = lambda a: pl.BlockSpec(a.shape, lambda i: (0,) * a.ndim)
    row = lambda w: pl.BlockSpec((tm, w), lambda i: (i, 0))
    return pl.pallas_call(
        functools.partial(_outproj_kernel, alpha),
        grid=(n // tm,),
        in_specs=[row(d), row(D_ATT), row(D_SSM), row(D_POOL), full(wo), full(ln_g), full(ln_b),
                  full(wr_t), full(br)],
        out_specs=[row(d), pl.BlockSpec((N_EXPERTS, tm), lambda i: (0, i))],
        out_shape=[jax.ShapeDtypeStruct((n, d), F32), jax.ShapeDtypeStruct((N_EXPERTS, n), F32)],
        compiler_params=_cparams(1),
        name="outproj_ln_router",
    )(x, att, ys, yq, wo, ln_g, ln_b, wr_t, br)


def _moe_kernel(alpha, d_expert, h_ref, gates_ref, wg_ref, wu_ref, wd_ref, g_ref, b_ref, o_ref,
                hb_ref, acc_ref):
    g = pl.program_id(1)

    @pl.when(g == 0)
    def _():
        hb_ref[...] = h_ref[...].astype(BF16)
        acc_ref[...] = jnp.zeros_like(acc_ref)

    hb = hb_ref[...]
    a = _dot(hb, wg_ref[...])
    b = _dot(hb, wu_ref[...])
    gates = gates_ref[0]
    parts = []
    for e in range(EXPERTS_PER_GROUP):
        cs = slice(e * d_expert, (e + 1) * d_expert)
        hid = jax.nn.silu(a[:, cs]) * b[:, cs] * gates[:, e:e + 1]
        parts.append(hid.astype(BF16))
    acc_ref[...] += _dot(jnp.concatenate(parts, axis=1), wd_ref[...])

    @pl.when(g == N_EXPERT_GROUPS - 1)
    def _():
        y = alpha * h_ref[...] + acc_ref[...]
        o_ref[...] = _layer_norm(y, g_ref[...], b_ref[...])


def _moe(h, gates_g, wg, wu, wd, ln_g, ln_b, alpha, tm):
    n, d = h.shape
    gw = wg.shape[1] // N_EXPERT_GROUPS
    d_expert = gw // EXPERTS_PER_GROUP
    full = lambda a: pl.BlockSpec(a.shape, lambda i, j: (0,) * a.ndim)
    return pl.pallas_call(
        functools.partial(_moe_kernel, alpha, d_expert),
        grid=(n // tm, N_EXPERT_GROUPS),
        in_specs=[pl.BlockSpec((tm, d), lambda i, j: (i, 0)),
                  pl.BlockSpec((1, tm, EXPERTS_PER_GROUP), lambda i, j: (j, i, 0)),
                  pl.BlockSpec((d, gw), lambda i, j: (0, j)),
                  pl.BlockSpec((d, gw), lambda i, j: (0, j)),
                  pl.BlockSpec((gw, d), lambda i, j: (j, 0)),
                  full(ln_g), full(ln_b)],
        out_specs=pl.BlockSpec((tm, d), lambda i, j: (i, 0)),
        out_shape=jax.ShapeDtypeStruct((n, d), F32),
        scratch_shapes=[pltpu.VMEM((tm, d), BF16), pltpu.VMEM((tm, d), F32)],
        compiler_params=_cparams(2),
        name="moe_ln",
    )(h, gates_g, wg, wu, wd, ln_g, ln_b)


def _post_block(x, att, ys, yq, lw, alpha, tm):
    n = x.shape[0]
    h, gates_t = _outproj(x, att, ys, yq, lw["wo"], lw["ln1_g"], lw["ln1_b"], lw["wr_t"], lw["br"], alpha, tm)
    gates_g = gates_t.reshape(N_EXPERT_GROUPS, EXPERTS_PER_GROUP, n).transpose(0, 2, 1)
    return _moe(h, gates_g, lw["wg"], lw["wu"], lw["wd"], lw["ln2_g"], lw["ln2_b"], alpha, tm)


def kernel(x_prompt, x_sample, cache_k, cache_v, cache_logf, page_table, state_ssm_re, state_ssm_im, state_pool, meta_tokens, ln0_g, ln0_b, w_in, b_forget, ssm_lam_re, ssm_lam_im, ssm_log_dt, ssm_b_re, ssm_b_im, ssm_c_re, ssm_c_im, ssm_d, w_glu, b_glu, w_pool, s_pool, w_out, ln1_g, ln1_b, w_router, b_router, w_gate, w_up, w_down, ln2_g, ln2_b):
    bp, seq, d_model = x_prompt.shape
    bs, t_new, _ = x_sample.shape
    depth = w_in.shape[0]
    n_pages, page = page_table.shape[1], cache_k.shape[2]
    past_len = n_pages * page
    length = N_META + seq
    lp = _round_up(length, LANES)
    tile = _pick_tile(lp, (640, 512, 384, 256, 128))
    alpha = (2 * depth) ** 0.25
    n_s = bs * t_new
    tm_s = _pick_tile(n_s, (512, 256, 128, 64, 32, 16, 8))
    assert t_new == SUBLANES and page == LANES and past_len >= POOL_BUF
    assert bs % DECODE_SEQS_PER_STEP == 0

    o1, o2, o3 = D_ATT, 2 * D_ATT, 3 * D_ATT
    o4 = o3 + N_HEADS
    row2 = lambda v: v.reshape(1, -1)

    kt_pages = cache_k.transpose(0, 1, 3, 4, 2).reshape(-1, depth, D_ATT, page)
    vt_pages = cache_v.transpose(0, 1, 3, 4, 2).reshape(-1, depth, D_ATT, page)
    lf_pages = cache_logf.transpose(0, 1, 3, 2)

    meta = jnp.broadcast_to(meta_tokens[None], (bp, N_META, d_model))
    xp = jnp.pad(x_prompt, ((0, 0), (N_META, lp - length), (0, 0)))
    xp = lax.dynamic_update_slice(xp, meta, (0, 0, 0))
    xs = x_sample.reshape(n_s, d_model)
    wr_hi = w_router.T.astype(BF16)
    wr_lo = (w_router.T - wr_hi.astype(F32)).astype(BF16)
    wr_t = jnp.concatenate([wr_hi, wr_lo], axis=0)
    br = b_router.reshape(N_EXPERTS, 1)
    nb_s = _pick_tile(bs, (16, 8, 4, 2, 1))

    outs_p = {k: [] for k in ("k", "v", "lf", "sr", "si", "pb")}
    outs_s = {k: [] for k in ("k", "v", "lf", "sr", "si", "pb")}
    for l in range(depth):
        w = w_in[l]
        wb = w.astype(BF16)
        wq = wb[:, :o1]
        wup = wb[:, o4:]
        wt = jnp.concatenate([wb[:, o1:o3], wb[:, o3:o4]], axis=1).T
        w_nat = jnp.concatenate([wb[:, :o3], wb[:, o4:]], axis=1)
        wf_t = wb[:, o3:o4].T
        bf = b_forget[l].reshape(N_HEADS, 1)

        apow_re, apow_im, bbr_t, bbi_t = _s5_params(ssm_lam_re[l], ssm_lam_im[l], ssm_log_dt[l],
                                                    ssm_b_re[l], ssm_b_im[l])
        apr = apow_re.reshape(SUBLANES, D_STATE)
        api = apow_im.reshape(SUBLANES, D_STATE)
        ridx = jnp.arange(SUBLANES)[:, None]
        ak = []
        for k in (1, 2, 4):
            ak.append(jnp.where(ridx >= k, apr[k - 1][None], 0.0))
            ak.append(jnp.where(ridx >= k, api[k - 1][None], 0.0))
        ak += [apr, api]
        mix_w = dict(
            bbr=_block_diag(bbr_t).astype(BF16), bbi=_block_diag(bbi_t).astype(BF16),
            cr=_block_diag(ssm_c_re[l].transpose(0, 2, 1)).astype(BF16),
            ci=_block_diag(ssm_c_im[l].transpose(0, 2, 1)).astype(BF16),
            d=row2(ssm_d[l]), ak=jnp.stack(ak), wglu=w_glu[l].astype(BF16), bglu=row2(b_glu[l]),
            wpool=_block_diag(w_pool[l]).astype(BF16), spool=row2(s_pool[l]))
        d_expert = w_gate.shape[-1]
        lw = dict(
            wo=w_out[l].astype(BF16), ln1_g=row2(ln1_g[l]), ln1_b=row2(ln1_b[l]), wr_t=wr_t, br=br,
            wg=w_gate[l].transpose(1, 0, 2).reshape(d_model, N_EXPERTS * d_expert).astype(BF16),
            wu=w_up[l].transpose(1, 0, 2).reshape(d_model, N_EXPERTS * d_expert).astype(BF16),
            wd=w_down[l].reshape(N_EXPERTS * d_expert, d_model).astype(BF16),
            ln2_g=row2(ln2_g[l]), ln2_b=row2(ln2_b[l]))

        res = _inproj_prompt(xp, row2(ln0_g), row2(ln0_b), wq, wup, wt, bf, l == 0, tile)
        if l == 0:
            xp, *res = res
        q, kt, vt, ka, vb, lft, u, p = res
        att = _attn_prompt(q, ka, vb, tile)
        ys, yq, sr, si = _seqmix(u, p, mix_w, length, tile)
        xp = _post_block(xp.reshape(bp * lp, d_model), att.reshape(bp * lp, D_ATT),
                         ys.reshape(bp * lp, D_SSM), yq.reshape(bp * lp, D_POOL), lw, alpha,
                         tile).reshape(bp, lp, d_model)
        outs_p["k"].append(kt[:, :, :length])
        outs_p["v"].append(vt[:, :, :length])
        outs_p["lf"].append(lft[:, :, :length])
        outs_p["sr"].append(sr.reshape(bp, N_SSM_GROUPS, SSM_STATE))
        outs_p["si"].append(si.reshape(bp, N_SSM_GROUPS, SSM_STATE))
        outs_p["pb"].append(p[:, length - POOL_BUF:length])

        res = _inproj_sample(xs, row2(ln0_g), row2(ln0_b), w_nat, wf_t, bf, l == 0, tm_s)
        if l == 0:
            xs, *res = res
        qkv, up, lfs_t = res
        lf_new = lfs_t.reshape(N_HEADS, bs, t_new).transpose(1, 0, 2)
        lf_new_pad = jnp.pad(lf_new, ((0, 0), (0, 0), (0, LANES - t_new)))
        att = _attn_decode(page_table, qkv.reshape(bs, t_new, 3 * D_ATT), lf_new_pad,
                           kt_pages, vt_pages, lf_pages, l)
        hist = jnp.concatenate([jnp.zeros((bs, 1, D_POOL), F32), state_pool[:, l]], axis=1)
        up3 = up.reshape(bs, t_new, D_SSM + D_POOL)
        p_s = up3[..., D_SSM:]
        ys, yq, sr, si = _seqmix_sample(up3, state_ssm_re[:, l].reshape(bs, 1, D_STATE),
                                        state_ssm_im[:, l].reshape(bs, 1, D_STATE), hist, mix_w, past_len,
                                        nb_s)
        xs = _post_block(xs, att.reshape(n_s, D_ATT), ys.reshape(n_s, D_SSM), yq.reshape(n_s, D_POOL),
                         lw, alpha, tm_s)
        qkv3 = qkv.reshape(bs, t_new, 3, N_HEADS, HEAD_DIM)
        outs_s["k"].append(qkv3[:, :, 1])
        outs_s["v"].append(qkv3[:, :, 2])
        outs_s["lf"].append(lf_new.transpose(0, 2, 1))
        outs_s["sr"].append(sr.reshape(bs, N_SSM_GROUPS, SSM_STATE))
        outs_s["si"].append(si.reshape(bs, N_SSM_GROUPS, SSM_STATE))
        outs_s["pb"].append(jnp.concatenate([state_pool[:, l], p_s], axis=1)[:, -POOL_BUF:])

    def heads_last(xs_t):
        a = jnp.stack(xs_t, axis=1)
        b_, dep, _, ln = a.shape
        return a.reshape(b_, dep, N_HEADS, HEAD_DIM, ln).transpose(0, 1, 4, 2, 3)

    y_prompt = xp[:, N_META:length]
    y_sample = xs.reshape(bs, t_new, d_model)
    return (y_prompt, y_sample,
            heads_last(outs_p["k"]), heads_last(outs_p["v"]),
            jnp.stack(outs_p["lf"], axis=1).transpose(0, 1, 3, 2),
            jnp.stack(outs_p["sr"], axis=1), jnp.stack(outs_p["si"], axis=1), jnp.stack(outs_p["pb"], axis=1),
            jnp.stack(outs_s["k"], axis=1), jnp.stack(outs_s["v"], axis=1), jnp.stack(outs_s["lf"], axis=1),
            jnp.stack(outs_s["sr"], axis=1), jnp.stack(outs_s["si"], axis=1), jnp.stack(outs_s["pb"], axis=1))
```

```python
import functools
import math

import jax
import jax.numpy as jnp
from jax import lax
from jax.experimental import pallas as pl
from jax.experimental.pallas import tpu as pltpu

F32 = jnp.float32
BF16 = jnp.bfloat16

N_META = 16
N_HEADS = 8
HEAD_DIM = 64
D_ATT = N_HEADS * HEAD_DIM
SSM_GROUP = 16
N_SSM_GROUPS = 16
SSM_STATE = 64
D_SSM = SSM_GROUP * N_SSM_GROUPS
D_STATE = N_SSM_GROUPS * SSM_STATE
POOL_WINDOWS = (2, 4, 8, 16)
POOL_GROUP = 64
D_POOL = POOL_GROUP * len(POOL_WINDOWS)
POOL_BUF = max(POOL_WINDOWS) - 1
POOL_HIST = POOL_BUF + 1
N_EXPERTS = 16
N_EXPERT_GROUPS = 4
EXPERTS_PER_GROUP = N_EXPERTS // N_EXPERT_GROUPS
TOP_K = 2
SCALE = HEAD_DIM ** -0.5
LOG2E = math.log2(math.e)
LN_EPS = 1e-5
NEG_INF = -1e30
HEADS_PER_STEP = 2
BIAS_PIECES = 3
SOFTMAX_ROWS = 32
DECODE_SEQS_PER_STEP = 2

LANES = 128
SUBLANES = 8
VMEM_LIMIT = 56 * 1024 * 1024


def _cparams(n_axes):
    return pltpu.CompilerParams(
        dimension_semantics=("arbitrary",) * n_axes, vmem_limit_bytes=VMEM_LIMIT)


def _round_up(x, m):
    return (x + m - 1) // m * m


def _pick_tile(n, candidates):
    for c in candidates:
        if n % c == 0:
            return c
    raise ValueError(f"no tile for {n}")


def _layer_norm(x, g, b):
    mu = jnp.mean(x, axis=-1, keepdims=True)
    xc = x - mu
    var = jnp.mean(xc * xc, axis=-1, keepdims=True)
    return xc * lax.rsqrt(var + LN_EPS) * g + b


def _log_sigmoid(x):
    return -(jnp.maximum(-x, 0.0) + jnp.log1p(jnp.exp(-jnp.abs(x))))


def _dot(a, b):
    return jnp.dot(a, b, preferred_element_type=F32)


def _dot_nt(a, b, precision=None):
    return lax.dot_general(a, b, (((1,), (1,)), ((), ())),
                           preferred_element_type=F32, precision=precision)


def _lane_cumsum(x):
    lane = lax.broadcasted_iota(jnp.int32, x.shape, 1)
    s = 1
    while s < LANES:
        x = x + jnp.where(lane >= s, pltpu.roll(x, s, axis=1), 0.0)
        s *= 2
    return x


def _s5_param_kernel(lr_ref, li_ref, ldt_ref, br_ref, bi_ref, apow_re, apow_im, bbr_ref, bbi_ref):
    lr = lr_ref[...]
    li = li_ref[...]
    dt = jnp.exp(ldt_ref[...])
    mag = jnp.exp(lr * dt)
    ab_re = mag * jnp.cos(li * dt)
    ab_im = mag * jnp.sin(li * dt)
    nr, ni = ab_re - 1.0, ab_im
    den = lr * lr + li * li
    gr = (nr * lr + ni * li) / den
    gi = (ni * lr - nr * li) / den
    br = br_ref[...]
    bi = bi_ref[...]
    bbr_ref[...] = gr[:, None, :] * br - gi[:, None, :] * bi
    bbi_ref[...] = gr[:, None, :] * bi + gi[:, None, :] * br
    pr, pi = ab_re, ab_im
    apow_re[0] = pr
    apow_im[0] = pi
    for k in range(1, SUBLANES):
        pr, pi = pr * ab_re - pi * ab_im, pr * ab_im + pi * ab_re
        apow_re[k] = pr
        apow_im[k] = pi


def _s5_params(lam_re, lam_im, log_dt, b_re, b_im):
    g, p = lam_re.shape
    c = b_re.shape[-1]
    out_shape = (jax.ShapeDtypeStruct((SUBLANES, g, p), F32), jax.ShapeDtypeStruct((SUBLANES, g, p), F32),
                 jax.ShapeDtypeStruct((g, c, p), F32), jax.ShapeDtypeStruct((g, c, p), F32))
    return pl.pallas_call(_s5_param_kernel, out_shape=out_shape, name="s5_params")(
        lam_re, lam_im, log_dt.reshape(g, 1), b_re.transpose(0, 2, 1), b_im.transpose(0, 2, 1))


def _block_diag(w):
    g, a, b = w.shape
    eye = jnp.eye(g, dtype=w.dtype)
    return (w[:, :, None, :] * eye[:, None, :, None]).reshape(g * a, g * b)


def _inproj_prompt_kernel(apply_ln, tm, x_ref, g_ref, b_ref, wq_ref, wup_ref, wt_ref, bf_ref, *outs):
    if apply_ln:
        xn_ref, q_ref, kt_ref, vt_ref, ka_ref, vb_ref, lft_ref, u_ref, p_ref, carry_ref, ck_scr = outs
    else:
        q_ref, kt_ref, vt_ref, ka_ref, vb_ref, lft_ref, u_ref, p_ref, carry_ref, ck_scr = outs
    t = pl.program_id(1)
    x = x_ref[0]
    if apply_ln:
        x = _layer_norm(x, g_ref[...], b_ref[...])
        xn_ref[0] = x
    xb = x.astype(BF16)
    q = _dot(xb, wq_ref[...]) * (SCALE * LOG2E)
    lane = lax.broadcasted_iota(jnp.int32, (tm, LANES), 1)
    ones = jnp.where(lane < HEAD_DIM + BIAS_PIECES, 1.0, 0.0)
    for h in range(N_HEADS):
        pair = q[:, (h // 2) * LANES:(h // 2 + 1) * LANES]
        if h % 2:
            pair = pltpu.roll(pair, HEAD_DIM, axis=1)
        q_ref[0, h] = jnp.where(lane < HEAD_DIM, pair, ones).astype(BF16)
    up = _dot(xb, wup_ref[...])
    u_ref[0] = up[:, :D_SSM]
    p_ref[0] = up[:, D_SSM:]
    zt = _dot_nt(wt_ref[...], xb)
    kt = zt[:D_ATT]
    vt = zt[D_ATT:2 * D_ATT]
    kt_ref[0] = kt
    vt_ref[0] = vt
    ones_row = lax.broadcasted_iota(jnp.int32, (HEAD_DIM, tm), 0) == 0
    for h in range(N_HEADS):
        vb_ref[0, h, 0:HEAD_DIM, :] = vt[h * HEAD_DIM:(h + 1) * HEAD_DIM].astype(BF16)
        vb_ref[0, h, HEAD_DIM:, :] = jnp.where(ones_row, 1.0, 0.0).astype(BF16)
    lf = _log_sigmoid(zt[2 * D_ATT:] + bf_ref[...])
    lft_ref[0] = lf

    @pl.when(t == 0)
    def _():
        carry_ref[...] = jnp.zeros_like(carry_ref)

    carry = carry_ref[...]
    for c in range(tm // LANES):
        blk = _lane_cumsum(lf[:, c * LANES:(c + 1) * LANES]) + carry
        ck_scr[:, c * LANES:(c + 1) * LANES] = blk
        carry = jnp.broadcast_to(blk[:, LANES - 1:LANES], carry.shape)
    carry_ref[...] = carry

    piece_row = lax.broadcasted_iota(jnp.int32, (HEAD_DIM, tm), 0)
    for h in range(N_HEADS):
        ka_ref[0, h, 0:HEAD_DIM, :] = kt[h * HEAD_DIM:(h + 1) * HEAD_DIM].astype(BF16)
        rest = ck_scr[h:h + 1, :] * (-LOG2E)
        aug = jnp.zeros((HEAD_DIM, tm), F32)
        for i in range(BIAS_PIECES):
            piece = rest.astype(BF16).astype(F32)
            aug = jnp.where(piece_row == i, piece, aug)
            rest = rest - piece
        ka_ref[0, h, HEAD_DIM:, :] = aug.astype(BF16)


def _inproj_prompt(x, ln_g, ln_b, wq, wup, wt, bf, apply_ln, tm):
    b, lp, d = x.shape
    nt = lp // tm
    full = lambda shape: pl.BlockSpec(shape, lambda i, j: (0,) * len(shape))
    out_shape, out_specs = [], []
    if apply_ln:
        out_shape.append(jax.ShapeDtypeStruct((b, lp, d), F32))
        out_specs.append(pl.BlockSpec((1, tm, d), lambda i, j: (i, j, 0)))
    out_shape += [
        jax.ShapeDtypeStruct((b, N_HEADS, lp, LANES), BF16),
        jax.ShapeDtypeStruct((b, D_ATT, lp), F32),
        jax.ShapeDtypeStruct((b, D_ATT, lp), F32),
        jax.ShapeDtypeStruct((b, N_HEADS, LANES, lp), BF16),
        jax.ShapeDtypeStruct((b, N_HEADS, LANES, lp), BF16),
        jax.ShapeDtypeStruct((b, N_HEADS, lp), F32),
        jax.ShapeDtypeStruct((b, lp, D_SSM), F32),
        jax.ShapeDtypeStruct((b, lp, D_POOL), F32),
    ]
    out_specs += [
        pl.BlockSpec((1, N_HEADS, tm, LANES), lambda i, j: (i, 0, j, 0)),
        pl.BlockSpec((1, D_ATT, tm), lambda i, j: (i, 0, j)),
        pl.BlockSpec((1, D_ATT, tm), lambda i, j: (i, 0, j)),
        pl.BlockSpec((1, N_HEADS, LANES, tm), lambda i, j: (i, 0, 0, j)),
        pl.BlockSpec((1, N_HEADS, LANES, tm), lambda i, j: (i, 0, 0, j)),
        pl.BlockSpec((1, N_HEADS, tm), lambda i, j: (i, 0, j)),
        pl.BlockSpec((1, tm, D_SSM), lambda i, j: (i, j, 0)),
        pl.BlockSpec((1, tm, D_POOL), lambda i, j: (i, j, 0)),
    ]
    return pl.pallas_call(
        functools.partial(_inproj_prompt_kernel, apply_ln, tm),
        grid=(b, nt),
        in_specs=[pl.BlockSpec((1, tm, d), lambda i, j: (i, j, 0)),
                  full(ln_g.shape), full(ln_b.shape), full(wq.shape), full(wup.shape),
                  full(wt.shape), full(bf.shape)],
        out_specs=out_specs,
        out_shape=out_shape,
        scratch_shapes=[pltpu.VMEM((N_HEADS, LANES), F32), pltpu.VMEM((N_HEADS, tm), F32)],
        compiler_params=_cparams(2),
        name="inproj_prompt",
    )(x, ln_g, ln_b, wq, wup, wt, bf)


def _inproj_sample_kernel(apply_ln, x_ref, g_ref, b_ref, w_ref, wf_ref, bf_ref, *outs):
    if apply_ln:
        xn_ref, qkv_ref, up_ref, lft_ref = outs
    else:
        qkv_ref, up_ref, lft_ref = outs
    x = x_ref[...]
    if apply_ln:
        x = _layer_norm(x, g_ref[...], b_ref[...])
        xn_ref[...] = x
    xb = x.astype(BF16)
    z = _dot(xb, w_ref[...])
    qkv_ref[...] = z[:, :3 * D_ATT]
    up_ref[...] = z[:, 3 * D_ATT:]
    lft_ref[...] = _log_sigmoid(_dot_nt(wf_ref[...], xb) + bf_ref[...])


def _inproj_sample(x, ln_g, ln_b, w, wf, bf, apply_ln, tm):
    n, d = x.shape
    full = lambda shape: pl.BlockSpec(shape, lambda i: (0,) * len(shape))
    out_shape, out_specs = [], []
    if apply_ln:
        out_shape.append(jax.ShapeDtypeStruct((n, d), F32))
        out_specs.append(pl.BlockSpec((tm, d), lambda i: (i, 0)))
    out_shape += [jax.ShapeDtypeStruct((n, 3 * D_ATT), F32),
                  jax.ShapeDtypeStruct((n, D_SSM + D_POOL), F32),
                  jax.ShapeDtypeStruct((N_HEADS, n), F32)]
    out_specs += [pl.BlockSpec((tm, 3 * D_ATT), lambda i: (i, 0)),
                  pl.BlockSpec((tm, D_SSM + D_POOL), lambda i: (i, 0)),
                  pl.BlockSpec((N_HEADS, tm), lambda i: (0, i))]
    return pl.pallas_call(
        functools.partial(_inproj_sample_kernel, apply_ln),
        grid=(n // tm,),
        in_specs=[pl.BlockSpec((tm, d), lambda i: (i, 0)),
                  full(ln_g.shape), full(ln_b.shape), full(w.shape), full(wf.shape), full(bf.shape)],
        out_specs=out_specs,
        out_shape=out_shape,
        compiler_params=_cparams(1),
        name="inproj_sample",
    )(x, ln_g, ln_b, w, wf, bf)


def _attn_prompt_kernel(tq, q_ref, ka_ref, vb_ref, o_ref,
                        s_a, s_b, p_a, p_b, al_a, al_b, m_scr, acc_scr):
    qi = pl.program_id(2)
    rc = SOFTMAX_ROWS
    m_scr[...] = jnp.full(m_scr.shape, NEG_INF, F32)
    acc_scr[...] = jnp.zeros(acc_scr.shape, F32)

    def scores(k, s_buf, j):
        k0 = pl.multiple_of(k * tq, LANES)
        s_buf[j] = _dot(q_ref[0, j], ka_ref[0, j, :, pl.ds(k0, tq)])

    def chunk(s_buf, j, c, masked):
        rows = slice(c * rc, (c + 1) * rc)
        sc = s_buf[j, rows, :]
        if masked:
            row = lax.broadcasted_iota(jnp.int32, (rc, tq), 0)
            col = lax.broadcasted_iota(jnp.int32, (rc, tq), 1)
            sc = jnp.where(col <= row + c * rc, sc, NEG_INF)
        return rows, sc

    def row_max(s_buf, al_buf, j, masked):
        for c in range(tq // rc):
            rows, sc = chunk(s_buf, j, c, masked)
            m_old = m_scr[j, rows, :]
            m_new = jnp.maximum(m_old, jnp.max(sc, axis=1, keepdims=True))
            al_buf[j, rows, :] = jnp.exp2(m_old - m_new)
            m_scr[j, rows, :] = m_new

    def probs(s_buf, p_buf, j, masked):
        for c in range(tq // rc):
            rows, sc = chunk(s_buf, j, c, masked)
            x = sc - jnp.concatenate([m_scr[j, rows, :]] * (tq // LANES), axis=1)
            p_buf[j, rows, :] = jnp.exp2(x.astype(BF16))

    def values(k, p_buf, al_buf, j):
        k0 = pl.multiple_of(k * tq, LANES)
        acc_scr[j] = al_buf[j] * acc_scr[j] + _dot_nt(p_buf[j], vb_ref[0, j, :, pl.ds(k0, tq)])

    for j in range(HEADS_PER_STEP):
        scores(0, s_a, j)
    p_b[...] = jnp.zeros(p_b.shape, BF16)
    al_b[...] = jnp.ones(al_b.shape, F32)

    def stage(k, s_cur, s_nxt, p_cur, p_prev, al_cur, al_prev):
        k_prev = jnp.maximum(k - 1, 0)
        for j in range(HEADS_PER_STEP):
            scores(k + 1, s_nxt, j)
        for j in range(HEADS_PER_STEP):
            row_max(s_cur, al_cur, j, False)
        for j in range(HEADS_PER_STEP):
            probs(s_cur, p_cur, j, False)
        for j in range(HEADS_PER_STEP):
            values(k_prev, p_prev, al_prev, j)

    def body(k, carry):
        @pl.when(k % 2 == 0)
        def _():
            stage(k, s_a, s_b, p_a, p_b, al_a, al_b)

        @pl.when(k % 2 == 1)
        def _():
            stage(k, s_b, s_a, p_b, p_a, al_b, al_a)

        return carry

    lax.fori_loop(0, qi, body, 0)

    def drain(s_cur, p_cur, p_prev, al_cur, al_prev):
        k_prev = jnp.maximum(qi - 1, 0)
        for j in range(HEADS_PER_STEP):
            row_max(s_cur, al_cur, j, True)
        for j in range(HEADS_PER_STEP):
            values(k_prev, p_prev, al_prev, j)
            probs(s_cur, p_cur, j, True)
        for j in range(HEADS_PER_STEP):
            values(qi, p_cur, al_cur, j)

    @pl.when(qi % 2 == 0)
    def _():
        drain(s_a, p_a, p_b, al_a, al_b)

    @pl.when(qi % 2 == 1)
    def _():
        drain(s_b, p_b, p_a, al_b, al_a)

    outs = []
    for j in range(HEADS_PER_STEP):
        acc = acc_scr[j]
        outs.append(acc[:, 0:HEAD_DIM] / acc[:, HEAD_DIM:HEAD_DIM + 1])
    o_ref[0] = jnp.concatenate(outs, axis=1).astype(o_ref.dtype)


def _attn_prompt(q, ka, vb, tq):
    b, _, lp, _ = q.shape
    hp = N_HEADS // HEADS_PER_STEP
    rows = HEADS_PER_STEP * HEAD_DIM
    return pl.pallas_call(
        functools.partial(_attn_prompt_kernel, tq),
        grid=(b, hp, lp // tq),
        in_specs=[pl.BlockSpec((1, HEADS_PER_STEP, tq, LANES), lambda i, h, j: (i, h, j, 0)),
                  pl.BlockSpec((1, HEADS_PER_STEP, LANES, lp), lambda i, h, j: (i, h, 0, 0)),
                  pl.BlockSpec((1, HEADS_PER_STEP, LANES, lp), lambda i, h, j: (i, h, 0, 0))],
        out_specs=pl.BlockSpec((1, tq, rows), lambda i, h, j: (i, j, h)),
        out_shape=jax.ShapeDtypeStruct((b, lp, D_ATT), BF16),
        scratch_shapes=[pltpu.VMEM((HEADS_PER_STEP, tq, tq), F32), pltpu.VMEM((HEADS_PER_STEP, tq, tq), F32),
                        pltpu.VMEM((HEADS_PER_STEP, tq, tq), BF16), pltpu.VMEM((HEADS_PER_STEP, tq, tq), BF16),
                        pltpu.VMEM((HEADS_PER_STEP, tq, LANES), F32), pltpu.VMEM((HEADS_PER_STEP, tq, LANES), F32),
                        pltpu.VMEM((HEADS_PER_STEP, tq, LANES), F32),
                        pltpu.VMEM((HEADS_PER_STEP, tq, LANES), F32)],
        compiler_params=_cparams(3),
        name="attn_prompt",
    )(q, ka, vb)


def _attn_decode_kernel(n_pages, t_new, layer, n_steps, pt_ref, qkv_ref, lfn_ref, lf_ref, kt_hbm, vt_hbm,
                        o_ref, kbuf, vbuf, ksem, vsem):
    step = pl.program_id(0)
    slot = lax.rem(step, 2)
    n_cp = DECODE_SEQS_PER_STEP * n_pages

    def page_copies(s, slot_):
        cps = []
        for i in range(n_cp):
            page_id = pt_ref[s * n_cp + i]
            cps.append(pltpu.make_async_copy(kt_hbm.at[page_id, layer], kbuf.at[slot_, i], ksem.at[slot_]))
            cps.append(pltpu.make_async_copy(vt_hbm.at[page_id, layer], vbuf.at[slot_, i], vsem.at[slot_]))
        return cps

    @pl.when(step == 0)
    def _():
        for cp in page_copies(0, 0):
            cp.start()

    @pl.when(step + 1 < n_steps)
    def _():
        for cp in page_copies(step + 1, 1 - slot):
            cp.start()

    for cp in page_copies(step, slot):
        cp.wait()

    rows = t_new * N_HEADS
    head_of_row = lax.broadcasted_iota(jnp.int32, (N_HEADS, D_ATT), 0)
    head_of_col = lax.broadcasted_iota(jnp.int32, (N_HEADS, D_ATT), 1) // HEAD_DIM
    hmask = head_of_row == head_of_col
    omask = jnp.concatenate([hmask] * t_new, axis=0)
    key_t = lax.broadcasted_iota(jnp.int32, (rows, LANES), 1)
    qry_t = lax.broadcasted_iota(jnp.int32, (rows, LANES), 0) // N_HEADS
    pad = jnp.zeros((LANES - t_new, D_ATT), F32)

    def expand(c):
        return jnp.concatenate([c] * t_new, axis=0)

    for sq in range(DECODE_SEQS_PER_STEP):
        base = n_pages * sq
        pt0 = step * n_cp + base
        qkv = qkv_ref[sq]
        q = qkv[:, :D_ATT] * SCALE
        k_new = qkv[:, D_ATT:2 * D_ATT]
        v_new = qkv[:, 2 * D_ATT:]
        qbd = jnp.concatenate(
            [jnp.where(hmask, jnp.broadcast_to(q[t:t + 1], (N_HEADS, D_ATT)), 0.0) for t in range(t_new)],
            axis=0).astype(BF16)

        local = _lane_cumsum(jnp.concatenate([lf_ref[pt_ref[pt0 + j], 0] for j in range(n_pages)]
                                             + [lfn_ref[sq]], axis=0))
        off = jnp.zeros((N_HEADS, LANES), F32)
        s_blocks = []
        for j in range(n_pages):
            ck = local[j * N_HEADS:(j + 1) * N_HEADS] + off
            off = jnp.broadcast_to(ck[:, LANES - 1:LANES], off.shape)
            s = _dot(qbd, kbuf[slot, base + j].astype(BF16))
            s_blocks.append(s - expand(ck))
        k_pad = jnp.concatenate([k_new, pad], axis=0).astype(BF16)
        v_pad = jnp.concatenate([v_new, pad], axis=0).astype(BF16)
        ck_new = local[n_pages * N_HEADS:] + off
        s_new = _dot_nt(qbd, k_pad) - expand(ck_new)
        s_blocks.append(jnp.where(key_t <= qry_t, s_new, NEG_INF))

        m = s_blocks[0]
        for s in s_blocks[1:]:
            m = jnp.maximum(m, s)
        m = m.max(axis=1, keepdims=True)
        l = jnp.zeros((rows, LANES), F32)
        acc = jnp.zeros((rows, D_ATT), F32)
        for j, s in enumerate(s_blocks):
            p = jnp.exp(s - m)
            l = l + p
            if j < n_pages:
                acc = acc + _dot_nt(p.astype(BF16), vbuf[slot, base + j].astype(BF16))
            else:
                acc = acc + _dot(p.astype(BF16), v_pad)
        acc = acc / l.sum(axis=1, keepdims=True)
        acc = jnp.where(omask, acc, 0.0)
        o_ref[sq] = acc.reshape(t_new, N_HEADS, D_ATT).sum(axis=1).astype(o_ref.dtype)


def _attn_decode(page_table, qkv, lf_new_t, kt_pages, vt_pages, lf_pages, layer):
    bs, t_new, _ = qkv.shape
    n_pages = page_table.shape[1]
    page = kt_pages.shape[-1]
    sps = DECODE_SEQS_PER_STEP
    n_steps = bs // sps
    pt_flat = page_table.reshape(-1)
    in_specs = [pl.BlockSpec((sps, t_new, 3 * D_ATT), lambda i, pt: (i, 0, 0)),
                pl.BlockSpec((sps, N_HEADS, LANES), lambda i, pt: (i, 0, 0)),
                pl.BlockSpec((lf_pages.shape[0], 1, N_HEADS, page), lambda i, pt: (0, layer, 0, 0),
                             pipeline_mode=pl.Buffered(1)),
                pl.BlockSpec(memory_space=pl.ANY),
                pl.BlockSpec(memory_space=pl.ANY)]
    page_buf = pltpu.VMEM((2, sps * n_pages, D_ATT, page), F32)
    grid_spec = pltpu.PrefetchScalarGridSpec(
        num_scalar_prefetch=1, grid=(n_steps,), in_specs=in_specs,
        out_specs=pl.BlockSpec((sps, t_new, D_ATT), lambda i, pt: (i, 0, 0)),
        scratch_shapes=[page_buf, page_buf, pltpu.SemaphoreType.DMA((2,)), pltpu.SemaphoreType.DMA((2,))])
    return pl.pallas_call(
        functools.partial(_attn_decode_kernel, n_pages, t_new, layer, n_steps),
        grid_spec=grid_spec,
        out_shape=jax.ShapeDtypeStruct((bs, t_new, D_ATT), BF16),
        compiler_params=_cparams(1),
        name="attn_decode",
    )(pt_flat, qkv, lf_new_t, lf_pages, kt_pages, vt_pages)


def _s5_scan_rows(br, bi, cr, ci, ak_ref):
    for idx, k in enumerate((1, 2, 4)):
        akr = ak_ref[2 * idx]
        aki = ak_ref[2 * idx + 1]
        sr = pltpu.roll(br, k, axis=0)
        si = pltpu.roll(bi, k, axis=0)
        br, bi = br + akr * sr - aki * si, bi + akr * si + aki * sr
    pr = ak_ref[6]
    pi = ak_ref[7]
    return br + pr * cr - pi * ci, bi + pr * ci + pi * cr


def _s5_readout(xr, xi, u, cr_ref, ci_ref, d_ref, wglu_ref, bglu_ref):
    y = _dot(xr.astype(BF16), cr_ref[...]) - _dot(xi.astype(BF16), ci_ref[...]) + d_ref[...] * u
    y = jax.nn.gelu(y)
    return y * jax.nn.sigmoid(_dot(y.astype(BF16), wglu_ref[...]) + bglu_ref[...])


def _pool_window_sums(e):
    c2 = e + pltpu.roll(e, 1, axis=0)
    c4 = c2 + pltpu.roll(c2, 2, axis=0)
    c8 = c4 + pltpu.roll(c4, 4, axis=0)
    c16 = c8 + pltpu.roll(c8, 8, axis=0)
    return c2, c4, c8, c16


def _pool_mix(sums, tok, pos, wpool_ref, spool_ref):
    c2, c4, c8, c16 = sums
    grp = lax.broadcasted_iota(jnp.int32, tok.shape, 1) // POOL_GROUP
    win = jnp.where(grp == 0, c2, jnp.where(grp == 1, c4, jnp.where(grp == 2, c8, c16)))
    wsize = jnp.where(grp == 0, 2, jnp.where(grp == 1, 4, jnp.where(grp == 2, 8, 16)))
    cnt = jnp.minimum(wsize, pos + 1).astype(F32)
    mm = win / cnt - tok
    return _dot(mm.astype(BF16), wpool_ref[...]) * spool_ref[...]


def _seqmix_kernel(tt, snap_tile, snap_row,
                   u_ref, p_ref, bbr_ref, bbi_ref, cr_ref, ci_ref, d_ref, ak_ref, wglu_ref, bglu_ref,
                   wpool_ref, spool_ref,
                   ys_ref, yq_ref, sr_ref, si_ref,
                   xr_scr, xi_scr, car_r, car_i, ext_scr):
    t = pl.program_id(1)

    @pl.when(t == 0)
    def _():
        car_r[...] = jnp.zeros_like(car_r)
        car_i[...] = jnp.zeros_like(car_i)
        ext_scr[0:POOL_HIST, :] = jnp.zeros((POOL_HIST, D_POOL), F32)

    u = u_ref[0]
    ub = u.astype(BF16)
    xr_scr[...] = _dot(ub, bbr_ref[...])
    xi_scr[...] = _dot(ub, bbi_ref[...])

    def scan_block(i, carry):
        cr, ci = carry
        r0 = pl.multiple_of(i * SUBLANES, SUBLANES)
        xr, xi = _s5_scan_rows(xr_scr[pl.ds(r0, SUBLANES), :], xi_scr[pl.ds(r0, SUBLANES), :], cr, ci, ak_ref)
        xr_scr[pl.ds(r0, SUBLANES), :] = xr
        xi_scr[pl.ds(r0, SUBLANES), :] = xi
        return xr[SUBLANES - 1:SUBLANES], xi[SUBLANES - 1:SUBLANES]

    cr, ci = lax.fori_loop(0, tt // SUBLANES, scan_block, (car_r[...], car_i[...]))
    car_r[...] = cr
    car_i[...] = ci

    @pl.when(t == snap_tile)
    def _():
        sr_ref[0] = xr_scr[snap_row:snap_row + 1, :]
        si_ref[0] = xi_scr[snap_row:snap_row + 1, :]

    ys_ref[0] = _s5_readout(xr_scr[...], xi_scr[...], u, cr_ref, ci_ref, d_ref, wglu_ref,
                            bglu_ref).astype(ys_ref.dtype)

    ext_scr[POOL_HIST:POOL_HIST + tt, :] = p_ref[0]
    e = ext_scr[...]
    sums = [c[POOL_HIST:] for c in _pool_window_sums(e)]
    pos = t * tt + lax.broadcasted_iota(jnp.int32, (tt, D_POOL), 0)
    yq_ref[0] = _pool_mix(sums, e[POOL_HIST:], pos, wpool_ref, spool_ref).astype(yq_ref.dtype)
    ext_scr[0:POOL_HIST, :] = e[tt:tt + POOL_HIST]


def _mix_weights(prm):
    return (prm["bbr"], prm["bbi"], prm["cr"], prm["ci"], prm["d"], prm["ak"], prm["wglu"],
            prm["bglu"], prm["wpool"], prm["spool"])


def _seqmix(u, p, prm, length, tt):
    b, lp, _ = u.shape
    nt = lp // tt
    snap_tile, snap_row = (length - 1) // tt, (length - 1) % tt
    full = lambda a: pl.BlockSpec(a.shape, lambda i, j: (0,) * a.ndim)
    weights = _mix_weights(prm)
    return pl.pallas_call(
        functools.partial(_seqmix_kernel, tt, snap_tile, snap_row),
        grid=(b, nt),
        in_specs=[pl.BlockSpec((1, tt, D_SSM), lambda i, j: (i, j, 0)),
                  pl.BlockSpec((1, tt, D_POOL), lambda i, j: (i, j, 0))]
                 + [full(w) for w in weights],
        out_specs=[pl.BlockSpec((1, tt, D_SSM), lambda i, j: (i, j, 0)),
                   pl.BlockSpec((1, tt, D_POOL), lambda i, j: (i, j, 0)),
                   pl.BlockSpec((1, 1, D_STATE), lambda i, j: (i, 0, 0)),
                   pl.BlockSpec((1, 1, D_STATE), lambda i, j: (i, 0, 0))],
        out_shape=[jax.ShapeDtypeStruct((b, lp, D_SSM), BF16),
                   jax.ShapeDtypeStruct((b, lp, D_POOL), BF16),
                   jax.ShapeDtypeStruct((b, 1, D_STATE), F32),
                   jax.ShapeDtypeStruct((b, 1, D_STATE), F32)],
        scratch_shapes=[pltpu.VMEM((tt, D_STATE), F32), pltpu.VMEM((tt, D_STATE), F32),
                        pltpu.VMEM((1, D_STATE), F32), pltpu.VMEM((1, D_STATE), F32),
                        pltpu.VMEM((POOL_HIST + tt, D_POOL), F32)],
        compiler_params=_cparams(2),
        name="seqmix",
    )(u, p, *weights)


def _seqmix_sample_kernel(nb, pos0, up_ref, x0r_ref, x0i_ref, hist_ref,
                          bbr_ref, bbi_ref, cr_ref, ci_ref, d_ref, ak_ref, wglu_ref, bglu_ref,
                          wpool_ref, spool_ref,
                          ys_ref, yq_ref, sr_ref, si_ref, xr_scr, xi_scr):
    t_new = SUBLANES
    up = up_ref[...].reshape(nb * t_new, D_SSM + D_POOL)
    u = up[:, :D_SSM]
    ub = u.astype(BF16)
    xr_scr[...] = _dot(ub, bbr_ref[...])
    xi_scr[...] = _dot(ub, bbi_ref[...])

    def scan_block(i, carry):
        r0 = pl.multiple_of(i * SUBLANES, SUBLANES)
        xr, xi = _s5_scan_rows(xr_scr[pl.ds(r0, SUBLANES), :], xi_scr[pl.ds(r0, SUBLANES), :],
                               x0r_ref[i], x0i_ref[i], ak_ref)
        xr_scr[pl.ds(r0, SUBLANES), :] = xr
        xi_scr[pl.ds(r0, SUBLANES), :] = xi
        sr_ref[i] = xr[SUBLANES - 1:SUBLANES]
        si_ref[i] = xi[SUBLANES - 1:SUBLANES]
        return carry

    lax.fori_loop(0, nb, scan_block, 0)
    ys = _s5_readout(xr_scr[...], xi_scr[...], u, cr_ref, ci_ref, d_ref, wglu_ref, bglu_ref)
    ys_ref[...] = ys.reshape(nb, t_new, D_SSM).astype(ys_ref.dtype)

    tok = up[:, D_SSM:].reshape(nb, t_new, D_POOL)
    per_seq = POOL_HIST + t_new
    e = jnp.concatenate([hist_ref[...], tok], axis=1).reshape(nb * per_seq, D_POOL)
    sums = [c.reshape(nb, per_seq, D_POOL)[:, POOL_HIST:].reshape(nb * t_new, D_POOL)
            for c in _pool_window_sums(e)]
    pos = pos0 + lax.broadcasted_iota(jnp.int32, (nb, t_new, D_POOL), 1).reshape(nb * t_new, D_POOL)
    yq = _pool_mix(sums, up[:, D_SSM:], pos, wpool_ref, spool_ref)
    yq_ref[...] = yq.reshape(nb, t_new, D_POOL).astype(yq_ref.dtype)


def _seqmix_sample(up, x0r, x0i, hist, prm, pos0, nb):
    bs, t_new, width = up.shape
    full = lambda a: pl.BlockSpec(a.shape, lambda i: (0,) * a.ndim)
    seq = lambda r, w: pl.BlockSpec((nb, r, w), lambda i: (i, 0, 0))
    weights = _mix_weights(prm)
    return pl.pallas_call(
        functools.partial(_seqmix_sample_kernel, nb, pos0),
        grid=(bs // nb,),
        in_specs=[seq(t_new, width), seq(1, D_STATE), seq(1, D_STATE), seq(POOL_HIST, D_POOL)]
                 + [full(w) for w in weights],
        out_specs=[seq(t_new, D_SSM), seq(t_new, D_POOL), seq(1, D_STATE), seq(1, D_STATE)],
        out_shape=[jax.ShapeDtypeStruct((bs, t_new, D_SSM), BF16),
                   jax.ShapeDtypeStruct((bs, t_new, D_POOL), BF16),
                   jax.ShapeDtypeStruct((bs, 1, D_STATE), F32),
                   jax.ShapeDtypeStruct((bs, 1, D_STATE), F32)],
        scratch_shapes=[pltpu.VMEM((nb * t_new, D_STATE), F32), pltpu.VMEM((nb * t_new, D_STATE), F32)],
        compiler_params=_cparams(1),
        name="seqmix_sample",
    )(up, x0r, x0i, hist, *weights)


def _outproj_kernel(alpha, x_ref, att_ref, ys_ref, yq_ref, wo_ref, g_ref, b_ref, wr_ref, br_ref,
                    h_ref, gates_ref):
    y = alpha * x_ref[...]
    y = y + _dot(att_ref[...], wo_ref[0:D_ATT, :])
    y = y + _dot(ys_ref[...], wo_ref[D_ATT:D_ATT + D_SSM, :])
    y = y + _dot(yq_ref[...], wo_ref[D_ATT + D_SSM:, :])
    h = _layer_norm(y, g_ref[...], b_ref[...])
    h_ref[...] = h
    h_hi = h.astype(BF16)
    h_lo = (h - h_hi.astype(F32)).astype(BF16)
    both = _dot_nt(wr_ref[...], h_hi)
    logits = both[:N_EXPERTS] + both[N_EXPERTS:] + _dot_nt(wr_ref[0:N_EXPERTS, :], h_lo)
    aff = jax.nn.sigmoid(logits)
    biased = aff + br_ref[...]
    rows = [biased[e:e + 1, :] for e in range(N_EXPERTS)]

    def beats(a, ia, c, ic):
        return (a >= c) if ia < ic else (a > c)

    gscore = []
    for g in range(N_EXPERT_GROUPS):
        members = list(range(g * EXPERTS_PER_GROUP, (g + 1) * EXPERTS_PER_GROUP))
        total = jnp.zeros_like(rows[0])
        for e in members:
            rank = jnp.zeros_like(rows[0])
            for o in members:
                if o != e:
                    rank = rank + beats(rows[o], o, rows[e], e).astype(F32)
            total = total + jnp.where(rank < 2.0, rows[e], 0.0)
        gscore.append(total)
    in_group = []
    for g in range(N_EXPERT_GROUPS):
        lost = jnp.zeros_like(rows[0])
        for o in range(N_EXPERT_GROUPS):
            if o != g:
                lost = lost + beats(gscore[o], o, gscore[g], g).astype(F32)
        in_group.append(lost < 1.0)
    masked = [jnp.where(in_group[e // EXPERTS_PER_GROUP], rows[e], NEG_INF) for e in range(N_EXPERTS)]
    sel = []
    for e in range(N_EXPERTS):
        rank = jnp.zeros_like(rows[0])
        for o in range(N_EXPERTS):
            if o != e:
                rank = rank + beats(masked[o], o, masked[e], e).astype(F32)
        sel.append(jnp.where(rank < float(TOP_K), aff[e:e + 1, :], 0.0))
    denom = sel[0]
    for e in range(1, N_EXPERTS):
        denom = denom + sel[e]
    gates_ref[...] = jnp.concatenate(sel, axis=0) / denom


def _outproj(x, att, ys, yq, wo, ln_g, ln_b, wr_t, br, alpha, tm):
    n, d = x.shape
    full = lambda a: pl.BlockSpec(a.shape, lambda i: (0,) * a.ndim)
    row = lambda w: pl.BlockSpec((tm, w), lambda i: (i, 0))
    return pl.pallas_call(
        functools.partial(_outproj_kernel, alpha),
        grid=(n // tm,),
        in_specs=[row(d), row(D_ATT), row(D_SSM), row(D_POOL), full(wo), full(ln_g), full(ln_b),
                  full(wr_t), full(br)],
        out_specs=[row(d), pl.BlockSpec((N_EXPERTS, tm), lambda i: (0, i))],
        out_shape=[jax.ShapeDtypeStruct((n, d), F32), jax.ShapeDtypeStruct((N_EXPERTS, n), F32)],
        compiler_params=_cparams(1),
        name="outproj_ln_router",
    )(x, att, ys, yq, wo, ln_g, ln_b, wr_t, br)


def _moe_kernel(alpha, d_expert, h_ref, gates_ref, wg_ref, wu_ref, wd_ref, g_ref, b_ref, o_ref,
                hb_ref, acc_ref):
    g = pl.program_id(1)

    @pl.when(g == 0)
    def _():
        hb_ref[...] = h_ref[...].astype(BF16)
        acc_ref[...] = jnp.zeros_like(acc_ref)

    hb = hb_ref[...]
    a = _dot(hb, wg_ref[...])
    b = _dot(hb, wu_ref[...])
    gates = gates_ref[0]
    parts = []
    for e in range(EXPERTS_PER_GROUP):
        cs = slice(e * d_expert, (e + 1) * d_expert)
        hid = jax.nn.silu(a[:, cs]) * b[:, cs] * gates[:, e:e + 1]
        parts.append(hid.astype(BF16))
    acc_ref[...] += _dot(jnp.concatenate(parts, axis=1), wd_ref[...])

    @pl.when(g == N_EXPERT_GROUPS - 1)
    def _():
        y = alpha * h_ref[...] + acc_ref[...]
        o_ref[...] = _layer_norm(y, g_ref[...], b_ref[...])


def _moe(h, gates_g, wg, wu, wd, ln_g, ln_b, alpha, tm):
    n, d = h.shape
    gw = wg.shape[1] // N_EXPERT_GROUPS
    d_expert = gw // EXPERTS_PER_GROUP
    full = lambda a: pl.BlockSpec(a.shape, lambda i, j: (0,) * a.ndim)
    return pl.pallas_call(
        functools.partial(_moe_kernel, alpha, d_expert),
        grid=(n // tm, N_EXPERT_GROUPS),
        in_specs=[pl.BlockSpec((tm, d), lambda i, j: (i, 0)),
                  pl.BlockSpec((1, tm, EXPERTS_PER_GROUP), lambda i, j: (j, i, 0)),
                  pl.BlockSpec((d, gw), lambda i, j: (0, j)),
                  pl.BlockSpec((d, gw), lambda i, j: (0, j)),
                  pl.BlockSpec((gw, d), lambda i, j: (j, 0)),
                  full(ln_g), full(ln_b)],
        out_specs=pl.BlockSpec((tm, d), lambda i, j: (i, 0)),
        out_shape=jax.ShapeDtypeStruct((n, d), F32),
        scratch_shapes=[pltpu.VMEM((tm, d), BF16), pltpu.VMEM((tm, d), F32)],
        compiler_params=_cparams(2),
        name="moe_ln",
    )(h, gates_g, wg, wu, wd, ln_g, ln_b)


def _post_block(x, att, ys, yq, lw, alpha, tm):
    n = x.shape[0]
    h, gates_t = _outproj(x, att, ys, yq, lw["wo"], lw["ln1_g"], lw["ln1_b"], lw["wr_t"], lw["br"], alpha, tm)
    gates_g = gates_t.reshape(N_EXPERT_GROUPS, EXPERTS_PER_GROUP, n).transpose(0, 2, 1)
    return _moe(h, gates_g, lw["wg"], lw["wu"], lw["wd"], lw["ln2_g"], lw["ln2_b"], alpha, tm)


def kernel(x_prompt, x_sample, cache_k, cache_v, cache_logf, page_table, state_ssm_re, state_ssm_im, state_pool, meta_tokens, ln0_g, ln0_b, w_in, b_forget, ssm_lam_re, ssm_lam_im, ssm_log_dt, ssm_b_re, ssm_b_im, ssm_c_re, ssm_c_im, ssm_d, w_glu, b_glu, w_pool, s_pool, w_out, ln1_g, ln1_b, w_router, b_router, w_gate, w_up, w_down, ln2_g, ln2_b):
    bp, seq, d_model = x_prompt.shape
    bs, t_new, _ = x_sample.shape
    depth = w_in.shape[0]
    n_pages, page = page_table.shape[1], cache_k.shape[2]
    past_len = n_pages * page
    length = N_META + seq
    lp = _round_up(length, LANES)
    tile = _pick_tile(lp, (640, 512, 384, 256, 128))
    alpha = (2 * depth) ** 0.25
    n_s = bs * t_new
    tm_s = _pick_tile(n_s, (512, 256, 128, 64, 32, 16, 8))
    assert t_new == SUBLANES and page == LANES and past_len >= POOL_BUF
    assert bs % DECODE_SEQS_PER_STEP == 0

    o1, o2, o3 = D_ATT, 2 * D_ATT, 3 * D_ATT
    o4 = o3 + N_HEADS
    row2 = lambda v: v.reshape(1, -1)

    kt_pages = cache_k.transpose(0, 1, 3, 4, 2).reshape(-1, depth, D_ATT, page)
    vt_pages = cache_v.transpose(0, 1, 3, 4, 2).reshape(-1, depth, D_ATT, page)
    lf_pages = cache_logf.transpose(0, 1, 3, 2)

    meta = jnp.broadcast_to(meta_tokens[None], (bp, N_META, d_model))
    xp = jnp.pad(x_prompt, ((0, 0), (N_META, lp - length), (0, 0)))
    xp = lax.dynamic_update_slice(xp, meta, (0, 0, 0))
    xs = x_sample.reshape(n_s, d_model)
    wr_hi = w_router.T.astype(BF16)
    wr_lo = (w_router.T - wr_hi.astype(F32)).astype(BF16)
    wr_t = jnp.concatenate([wr_hi, wr_lo], axis=0)
    br = b_router.reshape(N_EXPERTS, 1)
    nb_s = _pick_tile(bs, (16, 8, 4, 2, 1))

    outs_p = {k: [] for k in ("k", "v", "lf", "sr", "si", "pb")}
    outs_s = {k: [] for k in ("k", "v", "lf", "sr", "si", "pb")}
    for l in range(depth):
        w = w_in[l]
        wb = w.astype(BF16)
        wq = wb[:, :o1]
        wup = wb[:, o4:]
        wt = jnp.concatenate([wb[:, o1:o3], wb[:, o3:o4]], axis=1).T
        w_nat = jnp.concatenate([wb[:, :o3], wb[:, o4:]], axis=1)
        wf_t = wb[:, o3:o4].T
        bf = b_forget[l].reshape(N_HEADS, 1)

        apow_re, apow_im, bbr_t, bbi_t = _s5_params(ssm_lam_re[l], ssm_lam_im[l], ssm_log_dt[l],
                                                    ssm_b_re[l], ssm_b_im[l])
        apr = apow_re.reshape(SUBLANES, D_STATE)
        api = apow_im.reshape(SUBLANES, D_STATE)
        ridx = jnp.arange(SUBLANES)[:, None]
        ak = []
        for k in (1, 2, 4):
            ak.append(jnp.where(ridx >= k, apr[k - 1][None], 0.0))
            ak.append(jnp.where(ridx >= k, api[k - 1][None], 0.0))
        ak += [apr, api]
        mix_w = dict(
            bbr=_block_diag(bbr_t).astype(BF16), bbi=_block_diag(bbi_t).astype(BF16),
            cr=_block_diag(ssm_c_re[l].transpose(0, 2, 1)).astype(BF16),
            ci=_block_diag(ssm_c_im[l].transpose(0, 2, 1)).astype(BF16),
            d=row2(ssm_d[l]), ak=jnp.stack(ak), wglu=w_glu[l].astype(BF16), bglu=row2(b_glu[l]),
            wpool=_block_diag(w_pool[l]).astype(BF16), spool=row2(s_pool[l]))
        d_expert = w_gate.shape[-1]
        lw = dict(
            wo=w_out[l].astype(BF16), ln1_g=row2(ln1_g[l]), ln1_b=row2(ln1_b[l]), wr_t=wr_t, br=br,
            wg=w_gate[l].transpose(1, 0, 2).reshape(d_model, N_EXPERTS * d_expert).astype(BF16),
            wu=w_up[l].transpose(1, 0, 2).reshape(d_model, N_EXPERTS * d_expert).astype(BF16),
            wd=w_down[l].reshape(N_EXPERTS * d_expert, d_model).astype(BF16),
            ln2_g=row2(ln2_g[l]), ln2_b=row2(ln2_b[l]))

        res = _inproj_prompt(xp, row2(ln0_g), row2(ln0_b), wq, wup, wt, bf, l == 0, tile)
        if l == 0:
            xp, *res = res
        q, kt, vt, ka, vb, lft, u, p = res
        att = _attn_prompt(q, ka, vb, tile)
        ys, yq, sr, si = _seqmix(u, p, mix_w, length, tile)
        xp = _post_block(xp.reshape(bp * lp, d_model), att.reshape(bp * lp, D_ATT),
                         ys.reshape(bp * lp, D_SSM), yq.reshape(bp * lp, D_POOL), lw, alpha,
                         tile).reshape(bp, lp, d_model)
        outs_p["k"].append(kt[:, :, :length])
        outs_p["v"].append(vt[:, :, :length])
        outs_p["lf"].append(lft[:, :, :length])
        outs_p["sr"].append(sr.reshape(bp, N_SSM_GROUPS, SSM_STATE))
        outs_p["si"].append(si.reshape(bp, N_SSM_GROUPS, SSM_STATE))
        outs_p["pb"].append(p[:, length - POOL_BUF:length])

        res = _inproj_sample(xs, row2(ln0_g), row2(ln0_b), w_nat, wf_t, bf, l == 0, tm_s)
        if l == 0:
            xs, *res = res
        qkv, up, lfs_t = res
        lf_new = lfs_t.reshape(N_HEADS, bs, t_new).transpose(1, 0, 2)
        lf_new_pad = jnp.pad(lf_new, ((0, 0), (0, 0), (0, LANES - t_new)))
        att = _attn_decode(page_table, qkv.reshape(bs, t_new, 3 * D_ATT), lf_new_pad,
                           kt_pages, vt_pages, lf_pages, l)
        hist = jnp.concatenate([jnp.zeros((bs, 1, D_POOL), F32), state_pool[:, l]], axis=1)
        up3 = up.reshape(bs, t_new, D_SSM + D_POOL)
        p_s = up3[..., D_SSM:]
        ys, yq, sr, si = _seqmix_sample(up3, state_ssm_re[:, l].reshape(bs, 1, D_STATE),
                                        state_ssm_im[:, l].reshape(bs, 1, D_STATE), hist, mix_w, past_len,
                                        nb_s)
        xs = _post_block(xs, att.reshape(n_s, D_ATT), ys.reshape(n_s, D_SSM), yq.reshape(n_s, D_POOL),
                         lw, alpha, tm_s)
        qkv3 = qkv.reshape(bs, t_new, 3, N_HEADS, HEAD_DIM)
        outs_s["k"].append(qkv3[:, :, 1])
        outs_s["v"].append(qkv3[:, :, 2])
        outs_s["lf"].append(lf_new.transpose(0, 2, 1))
        outs_s["sr"].append(sr.reshape(bs, N_SSM_GROUPS, SSM_STATE))
        outs_s["si"].append(si.reshape(bs, N_SSM_GROUPS, SSM_STATE))
        outs_s["pb"].append(jnp.concatenate([state_pool[:, l], p_s], axis=1)[:, -POOL_BUF:])

    def heads_last(xs_t):
        a = jnp.stack(xs_t, axis=1)
        b_, dep, _, ln = a.shape
        return a.reshape(b_, dep, N_HEADS, HEAD_DIM, ln).transpose(0, 1, 4, 2, 3)

    y_prompt = xp[:, N_META:length]
    y_sample = xs.reshape(bs, t_new, d_model)
    return (y_prompt, y_sample,
            heads_last(outs_p["k"]), heads_last(outs_p["v"]),
            jnp.stack(outs_p["lf"], axis=1).transpose(0, 1, 3, 2),
            jnp.stack(outs_p["sr"], axis=1), jnp.stack(outs_p["si"], axis=1), jnp.stack(outs_p["pb"], axis=1),
            jnp.stack(outs_s["k"], axis=1), jnp.stack(outs_s["v"], axis=1), jnp.stack(outs_s["lf"], axis=1),
            jnp.stack(outs_s["sr"], axis=1), jnp.stack(outs_s["si"], axis=1), jnp.stack(outs_s["pb"], axis=1))
```

```python
import functools
import math

import jax
import jax.numpy as jnp
from jax import lax
from jax.experimental import pallas as pl
from jax.experimental.pallas import tpu as pltpu

F32 = jnp.float32
BF16 = jnp.bfloat16

N_META = 16
N_HEADS = 8
HEAD_DIM = 64
D_ATT = N_HEADS * HEAD_DIM
SSM_GROUP = 16
N_SSM_GROUPS = 16
SSM_STATE = 64
D_SSM = SSM_GROUP * N_SSM_GROUPS
D_STATE = N_SSM_GROUPS * SSM_STATE
POOL_WINDOWS = (2, 4, 8, 16)
POOL_GROUP = 64
D_POOL = POOL_GROUP * len(POOL_WINDOWS)
POOL_BUF = max(POOL_WINDOWS) - 1
POOL_HIST = POOL_BUF + 1
N_EXPERTS = 16
N_EXPERT_GROUPS = 4
EXPERTS_PER_GROUP = N_EXPERTS // N_EXPERT_GROUPS
TOP_K = 2
SCALE = HEAD_DIM ** -0.5
LOG2E = math.log2(math.e)
LN_EPS = 1e-5
NEG_INF = -1e30
HEADS_PER_STEP = 2
BIAS_PIECES = 3
SOFTMAX_ROWS = 32
DECODE_SEQS_PER_STEP = 2

LANES = 128
SUBLANES = 8
VMEM_LIMIT = 56 * 1024 * 1024


def _cparams(n_axes):
    return pltpu.CompilerParams(
        dimension_semantics=("arbitrary",) * n_axes, vmem_limit_bytes=VMEM_LIMIT)


def _round_up(x, m):
    return (x + m - 1) // m * m


def _shared_spec(a):
    return pl.BlockSpec(a.shape, lambda *_: (0,) * a.ndim)


def _layer_spec(a, layer):
    return pl.BlockSpec((None,) + a.shape[1:], lambda *_: (layer,) + (0,) * (a.ndim - 1))


def _pick_tile(n, candidates):
    for c in candidates:
        if n % c == 0:
            return c
    raise ValueError(f"no tile for {n}")


def _layer_norm(x, g, b):
    mu = jnp.mean(x, axis=-1, keepdims=True)
    xc = x - mu
    var = jnp.mean(xc * xc, axis=-1, keepdims=True)
    return xc * lax.rsqrt(var + LN_EPS) * g + b


def _log_sigmoid(x):
    return -(jnp.maximum(-x, 0.0) + jnp.log1p(jnp.exp(-jnp.abs(x))))


def _dot(a, b):
    return jnp.dot(a, b, preferred_element_type=F32)


def _dot_nt(a, b, precision=None):
    return lax.dot_general(a, b, (((1,), (1,)), ((), ())),
                           preferred_element_type=F32, precision=precision)


def _lane_cumsum(x):
    lane = lax.broadcasted_iota(jnp.int32, x.shape, 1)
    s = 1
    while s < LANES:
        x = x + jnp.where(lane >= s, pltpu.roll(x, s, axis=1), 0.0)
        s *= 2
    return x


def _s5_param_kernel(lr_ref, li_ref, ldt_ref, br_ref, bi_ref, apow_re, apow_im, bbr_ref, bbi_ref):
    lr = lr_ref[...]
    li = li_ref[...]
    dt = jnp.exp(ldt_ref[...])
    mag = jnp.exp(lr * dt)
    ab_re = mag * jnp.cos(li * dt)
    ab_im = mag * jnp.sin(li * dt)
    nr, ni = ab_re - 1.0, ab_im
    den = lr * lr + li * li
    gr = (nr * lr + ni * li) / den
    gi = (ni * lr - nr * li) / den
    br = br_ref[...]
    bi = bi_ref[...]
    bbr_ref[...] = gr[:, None, :] * br - gi[:, None, :] * bi
    bbi_ref[...] = gr[:, None, :] * bi + gi[:, None, :] * br
    pr, pi = ab_re, ab_im
    apow_re[0] = pr
    apow_im[0] = pi
    for k in range(1, SUBLANES):
        pr, pi = pr * ab_re - pi * ab_im, pr * ab_im + pi * ab_re
        apow_re[k] = pr
        apow_im[k] = pi


def _s5_params(lam_re, lam_im, log_dt, b_re, b_im):
    depth, g, p = lam_re.shape
    c = b_re.shape[-1]
    per_layer = lambda *dims: pl.BlockSpec((None,) + dims, lambda l: (l,) + (0,) * len(dims))
    out_shape = (jax.ShapeDtypeStruct((depth, SUBLANES, g, p), F32),
                 jax.ShapeDtypeStruct((depth, SUBLANES, g, p), F32),
                 jax.ShapeDtypeStruct((depth, g, c, p), F32), jax.ShapeDtypeStruct((depth, g, c, p), F32))
    return pl.pallas_call(
        _s5_param_kernel,
        grid=(depth,),
        in_specs=[per_layer(g, p), per_layer(g, p), per_layer(g, 1), per_layer(g, c, p), per_layer(g, c, p)],
        out_specs=[per_layer(SUBLANES, g, p), per_layer(SUBLANES, g, p), per_layer(g, c, p),
                   per_layer(g, c, p)],
        out_shape=out_shape,
        compiler_params=_cparams(1),
        name="s5_params",
    )(lam_re, lam_im, log_dt.reshape(depth, g, 1), b_re.transpose(0, 1, 3, 2), b_im.transpose(0, 1, 3, 2))


def _block_diag(w):
    dep, g, a, b = w.shape
    eye = jnp.eye(g, dtype=w.dtype)
    return (w[:, :, :, None, :] * eye[None, :, None, :, None]).reshape(dep, g * a, g * b)


def _inproj_prompt_kernel(apply_ln, tm, x_ref, g_ref, b_ref, wq_ref, wup_ref, wt_ref, bf_ref, *outs):
    if apply_ln:
        xn_ref, q_ref, kt_ref, vt_ref, ka_ref, vb_ref, lft_ref, u_ref, p_ref, carry_ref, ck_scr = outs
    else:
        q_ref, kt_ref, vt_ref, ka_ref, vb_ref, lft_ref, u_ref, p_ref, carry_ref, ck_scr = outs
    t = pl.program_id(1)
    x = x_ref[0]
    if apply_ln:
        x = _layer_norm(x, g_ref[...], b_ref[...])
        xn_ref[0] = x
    xb = x.astype(BF16)
    q = _dot(xb, wq_ref[...]) * (SCALE * LOG2E)
    lane = lax.broadcasted_iota(jnp.int32, (tm, LANES), 1)
    ones = jnp.where(lane < HEAD_DIM + BIAS_PIECES, 1.0, 0.0)
    for h in range(N_HEADS):
        pair = q[:, (h // 2) * LANES:(h // 2 + 1) * LANES]
        if h % 2:
            pair = pltpu.roll(pair, HEAD_DIM, axis=1)
        q_ref[0, h] = jnp.where(lane < HEAD_DIM, pair, ones).astype(BF16)
    up = _dot(xb, wup_ref[...])
    u_ref[0] = up[:, :D_SSM]
    p_ref[0] = up[:, D_SSM:]
    zt = _dot_nt(wt_ref[...], xb)
    kt = zt[:D_ATT]
    vt = zt[D_ATT:2 * D_ATT]
    kt_ref[0] = kt
    vt_ref[0] = vt
    ones_row = lax.broadcasted_iota(jnp.int32, (HEAD_DIM, tm), 0) == 0
    for h in range(N_HEADS):
        vb_ref[0, h, 0:HEAD_DIM, :] = vt[h * HEAD_DIM:(h + 1) * HEAD_DIM].astype(BF16)
        vb_ref[0, h, HEAD_DIM:, :] = jnp.where(ones_row, 1.0, 0.0).astype(BF16)
    lf = _log_sigmoid(zt[2 * D_ATT:] + bf_ref[...])
    lft_ref[0] = lf

    @pl.when(t == 0)
    def _():
        carry_ref[...] = jnp.zeros_like(carry_ref)

    carry = carry_ref[...]
    for c in range(tm // LANES):
        blk = _lane_cumsum(lf[:, c * LANES:(c + 1) * LANES]) + carry
        ck_scr[:, c * LANES:(c + 1) * LANES] = blk
        carry = jnp.broadcast_to(blk[:, LANES - 1:LANES], carry.shape)
    carry_ref[...] = carry

    piece_row = lax.broadcasted_iota(jnp.int32, (HEAD_DIM, tm), 0)
    for h in range(N_HEADS):
        ka_ref[0, h, 0:HEAD_DIM, :] = kt[h * HEAD_DIM:(h + 1) * HEAD_DIM].astype(BF16)
        rest = ck_scr[h:h + 1, :] * (-LOG2E)
        aug = jnp.zeros((HEAD_DIM, tm), F32)
        for i in range(BIAS_PIECES):
            piece = rest.astype(BF16).astype(F32)
            aug = jnp.where(piece_row == i, piece, aug)
            rest = rest - piece
        ka_ref[0, h, HEAD_DIM:, :] = aug.astype(BF16)


def _inproj_prompt(x, ln_g, ln_b, wq, wup, wt, bf, layer, apply_ln, tm):
    b, lp, d = x.shape
    nt = lp // tm
    out_shape, out_specs = [], []
    if apply_ln:
        out_shape.append(jax.ShapeDtypeStruct((b, lp, d), F32))
        out_specs.append(pl.BlockSpec((1, tm, d), lambda i, j: (i, j, 0)))
    out_shape += [
        jax.ShapeDtypeStruct((b, N_HEADS, lp, LANES), BF16),
        jax.ShapeDtypeStruct((b, D_ATT, lp), F32),
        jax.ShapeDtypeStruct((b, D_ATT, lp), F32),
        jax.ShapeDtypeStruct((b, N_HEADS, LANES, lp), BF16),
        jax.ShapeDtypeStruct((b, N_HEADS, LANES, lp), BF16),
        jax.ShapeDtypeStruct((b, N_HEADS, lp), F32),
        jax.ShapeDtypeStruct((b, lp, D_SSM), F32),
        jax.ShapeDtypeStruct((b, lp, D_POOL), F32),
    ]
    out_specs += [
        pl.BlockSpec((1, N_HEADS, tm, LANES), lambda i, j: (i, 0, j, 0)),
        pl.BlockSpec((1, D_ATT, tm), lambda i, j: (i, 0, j)),
        pl.BlockSpec((1, D_ATT, tm), lambda i, j: (i, 0, j)),
        pl.BlockSpec((1, N_HEADS, LANES, tm), lambda i, j: (i, 0, 0, j)),
        pl.BlockSpec((1, N_HEADS, LANES, tm), lambda i, j: (i, 0, 0, j)),
        pl.BlockSpec((1, N_HEADS, tm), lambda i, j: (i, 0, j)),
        pl.BlockSpec((1, tm, D_SSM), lambda i, j: (i, j, 0)),
        pl.BlockSpec((1, tm, D_POOL), lambda i, j: (i, j, 0)),
    ]
    return pl.pallas_call(
        functools.partial(_inproj_prompt_kernel, apply_ln, tm),
        grid=(b, nt),
        in_specs=[pl.BlockSpec((1, tm, d), lambda i, j: (i, j, 0)),
                  _shared_spec(ln_g), _shared_spec(ln_b), _layer_spec(wq, layer), _layer_spec(wup, layer),
                  _layer_spec(wt, layer), _layer_spec(bf, layer)],
        out_specs=out_specs,
        out_shape=out_shape,
        scratch_shapes=[pltpu.VMEM((N_HEADS, LANES), F32), pltpu.VMEM((N_HEADS, tm), F32)],
        compiler_params=_cparams(2),
        name="inproj_prompt",
    )(x, ln_g, ln_b, wq, wup, wt, bf)


def _inproj_sample_kernel(apply_ln, x_ref, g_ref, b_ref, w_ref, wf_ref, bf_ref, *outs):
    if apply_ln:
        xn_ref, qkv_ref, up_ref, lft_ref = outs
    else:
        qkv_ref, up_ref, lft_ref = outs
    x = x_ref[...]
    if apply_ln:
        x = _layer_norm(x, g_ref[...], b_ref[...])
        xn_ref[...] = x
    xb = x.astype(BF16)
    z = _dot(xb, w_ref[...])
    qkv_ref[...] = z[:, :3 * D_ATT]
    up_ref[...] = z[:, 3 * D_ATT:]
    lft_ref[...] = _log_sigmoid(_dot_nt(wf_ref[...], xb) + bf_ref[...])


def _inproj_sample(x, ln_g, ln_b, w, wf, bf, layer, apply_ln, tm):
    n, d = x.shape
    out_shape, out_specs = [], []
    if apply_ln:
        out_shape.append(jax.ShapeDtypeStruct((n, d), F32))
        out_specs.append(pl.BlockSpec((tm, d), lambda i: (i, 0)))
    out_shape += [jax.ShapeDtypeStruct((n, 3 * D_ATT), F32),
                  jax.ShapeDtypeStruct((n, D_SSM + D_POOL), F32),
                  jax.ShapeDtypeStruct((N_HEADS, n), F32)]
    out_specs += [pl.BlockSpec((tm, 3 * D_ATT), lambda i: (i, 0)),
                  pl.BlockSpec((tm, D_SSM + D_POOL), lambda i: (i, 0)),
                  pl.BlockSpec((N_HEADS, tm), lambda i: (0, i))]
    return pl.pallas_call(
        functools.partial(_inproj_sample_kernel, apply_ln),
        grid=(n // tm,),
        in_specs=[pl.BlockSpec((tm, d), lambda i: (i, 0)),
                  _shared_spec(ln_g), _shared_spec(ln_b), _layer_spec(w, layer), _layer_spec(wf, layer),
                  _layer_spec(bf, layer)],
        out_specs=out_specs,
        out_shape=out_shape,
        compiler_params=_cparams(1),
        name="inproj_sample",
    )(x, ln_g, ln_b, w, wf, bf)


def _attn_prompt_kernel(tq, q_ref, ka_ref, vb_ref, o_ref,
                        s_a, s_b, p_a, p_b, al_a, al_b, m_scr, acc_scr):
    qi = pl.program_id(2)
    rc = SOFTMAX_ROWS
    m_scr[...] = jnp.full(m_scr.shape, NEG_INF, F32)
    acc_scr[...] = jnp.zeros(acc_scr.shape, F32)

    def scores(k, s_buf, j):
        k0 = pl.multiple_of(k * tq, LANES)
        s_buf[j] = _dot(q_ref[0, j], ka_ref[0, j, :, pl.ds(k0, tq)])

    def chunk(s_buf, j, c, masked):
        rows = slice(c * rc, (c + 1) * rc)
        sc = s_buf[j, rows, :]
        if masked:
            row = lax.broadcasted_iota(jnp.int32, (rc, tq), 0)
            col = lax.broadcasted_iota(jnp.int32, (rc, tq), 1)
            sc = jnp.where(col <= row + c * rc, sc, NEG_INF)
        return rows, sc

    def row_max(s_buf, al_buf, j, masked):
        for c in range(tq // rc):
            rows, sc = chunk(s_buf, j, c, masked)
            m_old = m_scr[j, rows, :]
            m_new = jnp.maximum(m_old, jnp.max(sc, axis=1, keepdims=True))
            al_buf[j, rows, :] = jnp.exp2(m_old - m_new)
            m_scr[j, rows, :] = m_new

    def probs(s_buf, p_buf, j, masked):
        for c in range(tq // rc):
            rows, sc = chunk(s_buf, j, c, masked)
            x = sc - jnp.concatenate([m_scr[j, rows, :]] * (tq // LANES), axis=1)
            p_buf[j, rows, :] = jnp.exp2(x.astype(BF16))

    def values(k, p_buf, al_buf, j):
        k0 = pl.multiple_of(k * tq, LANES)
        acc_scr[j] = al_buf[j] * acc_scr[j] + _dot_nt(p_buf[j], vb_ref[0, j, :, pl.ds(k0, tq)])

    for j in range(HEADS_PER_STEP):
        scores(0, s_a, j)
    p_b[...] = jnp.zeros(p_b.shape, BF16)
    al_b[...] = jnp.ones(al_b.shape, F32)

    def stage(k, s_cur, s_nxt, p_cur, p_prev, al_cur, al_prev):
        k_prev = jnp.maximum(k - 1, 0)
        for j in range(HEADS_PER_STEP):
            scores(k + 1, s_nxt, j)
        for j in range(HEADS_PER_STEP):
            row_max(s_cur, al_cur, j, False)
        for j in range(HEADS_PER_STEP):
            probs(s_cur, p_cur, j, False)
        for j in range(HEADS_PER_STEP):
            values(k_prev, p_prev, al_prev, j)

    def body(k, carry):
        @pl.when(k % 2 == 0)
        def _():
            stage(k, s_a, s_b, p_a, p_b, al_a, al_b)

        @pl.when(k % 2 == 1)
        def _():
            stage(k, s_b, s_a, p_b, p_a, al_b, al_a)

        return carry

    lax.fori_loop(0, qi, body, 0)

    def drain(s_cur, p_cur, p_prev, al_cur, al_prev):
        k_prev = jnp.maximum(qi - 1, 0)
        for j in range(HEADS_PER_STEP):
            row_max(s_cur, al_cur, j, True)
        for j in range(HEADS_PER_STEP):
            values(k_prev, p_prev, al_prev, j)
            probs(s_cur, p_cur, j, True)
        for j in range(HEADS_PER_STEP):
            values(qi, p_cur, al_cur, j)

    @pl.when(qi % 2 == 0)
    def _():
        drain(s_a, p_a, p_b, al_a, al_b)

    @pl.when(qi % 2 == 1)
    def _():
        drain(s_b, p_b, p_a, al_b, al_a)

    outs = []
    for j in range(HEADS_PER_STEP):
        acc = acc_scr[j]
        outs.append(acc[:, 0:HEAD_DIM] / acc[:, HEAD_DIM:HEAD_DIM + 1])
    o_ref[0] = jnp.concatenate(outs, axis=1).astype(o_ref.dtype)


def _attn_prompt(q, ka, vb, tq):
    b, _, lp, _ = q.shape
    hp = N_HEADS // HEADS_PER_STEP
    rows = HEADS_PER_STEP * HEAD_DIM
    return pl.pallas_call(
        functools.partial(_attn_prompt_kernel, tq),
        grid=(b, hp, lp // tq),
        in_specs=[pl.BlockSpec((1, HEADS_PER_STEP, tq, LANES), lambda i, h, j: (i, h, j, 0)),
                  pl.BlockSpec((1, HEADS_PER_STEP, LANES, lp), lambda i, h, j: (i, h, 0, 0)),
                  pl.BlockSpec((1, HEADS_PER_STEP, LANES, lp), lambda i, h, j: (i, h, 0, 0))],
        out_specs=pl.BlockSpec((1, tq, rows), lambda i, h, j: (i, j, h)),
        out_shape=jax.ShapeDtypeStruct((b, lp, D_ATT), BF16),
        scratch_shapes=[pltpu.VMEM((HEADS_PER_STEP, tq, tq), F32), pltpu.VMEM((HEADS_PER_STEP, tq, tq), F32),
                        pltpu.VMEM((HEADS_PER_STEP, tq, tq), BF16), pltpu.VMEM((HEADS_PER_STEP, tq, tq), BF16),
                        pltpu.VMEM((HEADS_PER_STEP, tq, LANES), F32), pltpu.VMEM((HEADS_PER_STEP, tq, LANES), F32),
                        pltpu.VMEM((HEADS_PER_STEP, tq, LANES), F32),
                        pltpu.VMEM((HEADS_PER_STEP, tq, LANES), F32)],
        compiler_params=_cparams(3),
        name="attn_prompt",
    )(q, ka, vb)


def _attn_decode_kernel(n_pages, t_new, layer, n_steps, pt_ref, qkv_ref, lfn_ref, lf_ref, kt_hbm, vt_hbm,
                        o_ref, kbuf, vbuf, ksem, vsem):
    step = pl.program_id(0)
    slot = lax.rem(step, 2)
    n_cp = DECODE_SEQS_PER_STEP * n_pages

    def page_copies(s, slot_):
        cps = []
        for i in range(n_cp):
            page_id = pt_ref[s * n_cp + i]
            cps.append(pltpu.make_async_copy(kt_hbm.at[page_id, layer], kbuf.at[slot_, i], ksem.at[slot_]))
            cps.append(pltpu.make_async_copy(vt_hbm.at[page_id, layer], vbuf.at[slot_, i], vsem.at[slot_]))
        return cps

    @pl.when(step == 0)
    def _():
        for cp in page_copies(0, 0):
            cp.start()

    @pl.when(step + 1 < n_steps)
    def _():
        for cp in page_copies(step + 1, 1 - slot):
            cp.start()

    for cp in page_copies(step, slot):
        cp.wait()

    rows = t_new * N_HEADS
    head_of_row = lax.broadcasted_iota(jnp.int32, (N_HEADS, D_ATT), 0)
    head_of_col = lax.broadcasted_iota(jnp.int32, (N_HEADS, D_ATT), 1) // HEAD_DIM
    hmask = head_of_row == head_of_col
    omask = jnp.concatenate([hmask] * t_new, axis=0)
    key_t = lax.broadcasted_iota(jnp.int32, (rows, LANES), 1)
    qry_t = lax.broadcasted_iota(jnp.int32, (rows, LANES), 0) // N_HEADS
    pad = jnp.zeros((LANES - t_new, D_ATT), F32)

    def expand(c):
        return jnp.concatenate([c] * t_new, axis=0)

    for sq in range(DECODE_SEQS_PER_STEP):
        base = n_pages * sq
        pt0 = step * n_cp + base
        qkv = qkv_ref[sq]
        q = qkv[:, :D_ATT] * SCALE
        k_new = qkv[:, D_ATT:2 * D_ATT]
        v_new = qkv[:, 2 * D_ATT:]
        qbd = jnp.concatenate(
            [jnp.where(hmask, jnp.broadcast_to(q[t:t + 1], (N_HEADS, D_ATT)), 0.0) for t in range(t_new)],
            axis=0).astype(BF16)

        local = _lane_cumsum(jnp.concatenate([lf_ref[pt_ref[pt0 + j], 0] for j in range(n_pages)]
                                             + [lfn_ref[sq]], axis=0))
        off = jnp.zeros((N_HEADS, LANES), F32)
        s_blocks = []
        for j in range(n_pages):
            ck = local[j * N_HEADS:(j + 1) * N_HEADS] + off
            off = jnp.broadcast_to(ck[:, LANES - 1:LANES], off.shape)
            s = _dot(qbd, kbuf[slot, base + j].astype(BF16))
            s_blocks.append(s - expand(ck))
        k_pad = jnp.concatenate([k_new, pad], axis=0).astype(BF16)
        v_pad = jnp.concatenate([v_new, pad], axis=0).astype(BF16)
        ck_new = local[n_pages * N_HEADS:] + off
        s_new = _dot_nt(qbd, k_pad) - expand(ck_new)
        s_blocks.append(jnp.where(key_t <= qry_t, s_new, NEG_INF))

        m = s_blocks[0]
        for s in s_blocks[1:]:
            m = jnp.maximum(m, s)
        m = m.max(axis=1, keepdims=True)
        l = jnp.zeros((rows, LANES), F32)
        acc = jnp.zeros((rows, D_ATT), F32)
        for j, s in enumerate(s_blocks):
            p = jnp.exp(s - m)
            l = l + p
            if j < n_pages:
                acc = acc + _dot_nt(p.astype(BF16), vbuf[slot, base + j].astype(BF16))
            else:
                acc = acc + _dot(p.astype(BF16), v_pad)
        acc = acc / l.sum(axis=1, keepdims=True)
        acc = jnp.where(omask, acc, 0.0)
        o_ref[sq] = acc.reshape(t_new, N_HEADS, D_ATT).sum(axis=1).astype(o_ref.dtype)


def _attn_decode(page_table, qkv, lf_new_t, kt_pages, vt_pages, lf_pages, layer):
    bs, t_new, _ = qkv.shape
    n_pages = page_table.shape[1]
    page = kt_pages.shape[-1]
    sps = DECODE_SEQS_PER_STEP
    n_steps = bs // sps
    pt_flat = page_table.reshape(-1)
    in_specs = [pl.BlockSpec((sps, t_new, 3 * D_ATT), lambda i, pt: (i, 0, 0)),
                pl.BlockSpec((sps, N_HEADS, LANES), lambda i, pt: (i, 0, 0)),
                pl.BlockSpec((lf_pages.shape[0], 1, N_HEADS, page), lambda i, pt: (0, layer, 0, 0),
                             pipeline_mode=pl.Buffered(1)),
                pl.BlockSpec(memory_space=pl.ANY),
                pl.BlockSpec(memory_space=pl.ANY)]
    page_buf = pltpu.VMEM((2, sps * n_pages, D_ATT, page), F32)
    grid_spec = pltpu.PrefetchScalarGridSpec(
        num_scalar_prefetch=1, grid=(n_steps,), in_specs=in_specs,
        out_specs=pl.BlockSpec((sps, t_new, D_ATT), lambda i, pt: (i, 0, 0)),
        scratch_shapes=[page_buf, page_buf, pltpu.SemaphoreType.DMA((2,)), pltpu.SemaphoreType.DMA((2,))])
    return pl.pallas_call(
        functools.partial(_attn_decode_kernel, n_pages, t_new, layer, n_steps),
        grid_spec=grid_spec,
        out_shape=jax.ShapeDtypeStruct((bs, t_new, D_ATT), BF16),
        compiler_params=_cparams(1),
        name="attn_decode",
    )(pt_flat, qkv, lf_new_t, lf_pages, kt_pages, vt_pages)


def _s5_scan_rows(br, bi, cr, ci, ak_ref):
    for idx, k in enumerate((1, 2, 4)):
        akr = ak_ref[2 * idx]
        aki = ak_ref[2 * idx + 1]
        sr = pltpu.roll(br, k, axis=0)
        si = pltpu.roll(bi, k, axis=0)
        br, bi = br + akr * sr - aki * si, bi + akr * si + aki * sr
    pr = ak_ref[6]
    pi = ak_ref[7]
    return br + pr * cr - pi * ci, bi + pr * ci + pi * cr


def _s5_readout(xr, xi, u, cr_ref, ci_ref, d_ref, wglu_ref, bglu_ref):
    y = _dot(xr.astype(BF16), cr_ref[...]) - _dot(xi.astype(BF16), ci_ref[...]) + d_ref[...] * u
    y = jax.nn.gelu(y)
    return y * jax.nn.sigmoid(_dot(y.astype(BF16), wglu_ref[...]) + bglu_ref[...])


def _pool_window_sums(e):
    c2 = e + pltpu.roll(e, 1, axis=0)
    c4 = c2 + pltpu.roll(c2, 2, axis=0)
    c8 = c4 + pltpu.roll(c4, 4, axis=0)
    c16 = c8 + pltpu.roll(c8, 8, axis=0)
    return c2, c4, c8, c16


def _pool_mix(sums, tok, pos, wpool_ref, spool_ref):
    c2, c4, c8, c16 = sums
    grp = lax.broadcasted_iota(jnp.int32, tok.shape, 1) // POOL_GROUP
    win = jnp.where(grp == 0, c2, jnp.where(grp == 1, c4, jnp.where(grp == 2, c8, c16)))
    wsize = jnp.where(grp == 0, 2, jnp.where(grp == 1, 4, jnp.where(grp == 2, 8, 16)))
    cnt = jnp.minimum(wsize, pos + 1).astype(F32)
    mm = win / cnt - tok
    return _dot(mm.astype(BF16), wpool_ref[...]) * spool_ref[...]


def _seqmix_kernel(tt, snap_tile, snap_row,
                   u_ref, p_ref, bbr_ref, bbi_ref, cr_ref, ci_ref, d_ref, ak_ref, wglu_ref, bglu_ref,
                   wpool_ref, spool_ref,
                   ys_ref, yq_ref, sr_ref, si_ref,
                   xr_scr, xi_scr, car_r, car_i, ext_scr):
    t = pl.program_id(1)

    @pl.when(t == 0)
    def _():
        car_r[...] = jnp.zeros_like(car_r)
        car_i[...] = jnp.zeros_like(car_i)
        ext_scr[0:POOL_HIST, :] = jnp.zeros((POOL_HIST, D_POOL), F32)

    u = u_ref[0]
    ub = u.astype(BF16)
    xr_scr[...] = _dot(ub, bbr_ref[...])
    xi_scr[...] = _dot(ub, bbi_ref[...])

    def scan_block(i, carry):
        cr, ci = carry
        r0 = pl.multiple_of(i * SUBLANES, SUBLANES)
        xr, xi = _s5_scan_rows(xr_scr[pl.ds(r0, SUBLANES), :], xi_scr[pl.ds(r0, SUBLANES), :], cr, ci, ak_ref)
        xr_scr[pl.ds(r0, SUBLANES), :] = xr
        xi_scr[pl.ds(r0, SUBLANES), :] = xi
        return xr[SUBLANES - 1:SUBLANES], xi[SUBLANES - 1:SUBLANES]

    cr, ci = lax.fori_loop(0, tt // SUBLANES, scan_block, (car_r[...], car_i[...]))
    car_r[...] = cr
    car_i[...] = ci

    @pl.when(t == snap_tile)
    def _():
        sr_ref[0] = xr_scr[snap_row:snap_row + 1, :]
        si_ref[0] = xi_scr[snap_row:snap_row + 1, :]

    ys_ref[0] = _s5_readout(xr_scr[...], xi_scr[...], u, cr_ref, ci_ref, d_ref, wglu_ref,
                            bglu_ref).astype(ys_ref.dtype)

    ext_scr[POOL_HIST:POOL_HIST + tt, :] = p_ref[0]
    e = ext_scr[...]
    sums = [c[POOL_HIST:] for c in _pool_window_sums(e)]
    pos = t * tt + lax.broadcasted_iota(jnp.int32, (tt, D_POOL), 0)
    yq_ref[0] = _pool_mix(sums, e[POOL_HIST:], pos, wpool_ref, spool_ref).astype(yq_ref.dtype)
    ext_scr[0:POOL_HIST, :] = e[tt:tt + POOL_HIST]


def _mix_weights(prm):
    return (prm["bbr"], prm["bbi"], prm["cr"], prm["ci"], prm["d"], prm["ak"], prm["wglu"],
            prm["bglu"], prm["wpool"], prm["spool"])


def _seqmix(u, p, prm, layer, length, tt):
    b, lp, _ = u.shape
    nt = lp // tt
    snap_tile, snap_row = (length - 1) // tt, (length - 1) % tt
    full = lambda a: _layer_spec(a, layer)
    weights = _mix_weights(prm)
    return pl.pallas_call(
        functools.partial(_seqmix_kernel, tt, snap_tile, snap_row),
        grid=(b, nt),
        in_specs=[pl.BlockSpec((1, tt, D_SSM), lambda i, j: (i, j, 0)),
                  pl.BlockSpec((1, tt, D_POOL), lambda i, j: (i, j, 0))]
                 + [full(w) for w in weights],
        out_specs=[pl.BlockSpec((1, tt, D_SSM), lambda i, j: (i, j, 0)),
                   pl.BlockSpec((1, tt, D_POOL), lambda i, j: (i, j, 0)),
                   pl.BlockSpec((1, 1, D_STATE), lambda i, j: (i, 0, 0)),
                   pl.BlockSpec((1, 1, D_STATE), lambda i, j: (i, 0, 0))],
        out_shape=[jax.ShapeDtypeStruct((b, lp, D_SSM), BF16),
                   jax.ShapeDtypeStruct((b, lp, D_POOL), BF16),
                   jax.ShapeDtypeStruct((b, 1, D_STATE), F32),
                   jax.ShapeDtypeStruct((b, 1, D_STATE), F32)],
        scratch_shapes=[pltpu.VMEM((tt, D_STATE), F32), pltpu.VMEM((tt, D_STATE), F32),
                        pltpu.VMEM((1, D_STATE), F32), pltpu.VMEM((1, D_STATE), F32),
                        pltpu.VMEM((POOL_HIST + tt, D_POOL), F32)],
        compiler_params=_cparams(2),
        name="seqmix",
    )(u, p, *weights)


def _seqmix_sample_kernel(nb, pos0, up_ref, x0r_ref, x0i_ref, hist_ref,
                          bbr_ref, bbi_ref, cr_ref, ci_ref, d_ref, ak_ref, wglu_ref, bglu_ref,
                          wpool_ref, spool_ref,
                          ys_ref, yq_ref, sr_ref, si_ref, xr_scr, xi_scr):
    t_new = SUBLANES
    up = up_ref[...].reshape(nb * t_new, D_SSM + D_POOL)
    u = up[:, :D_SSM]
    ub = u.astype(BF16)
    xr_scr[...] = _dot(ub, bbr_ref[...])
    xi_scr[...] = _dot(ub, bbi_ref[...])

    def scan_block(i, carry):
        r0 = pl.multiple_of(i * SUBLANES, SUBLANES)
        xr, xi = _s5_scan_rows(xr_scr[pl.ds(r0, SUBLANES), :], xi_scr[pl.ds(r0, SUBLANES), :],
                               x0r_ref[i], x0i_ref[i], ak_ref)
        xr_scr[pl.ds(r0, SUBLANES), :] = xr
        xi_scr[pl.ds(r0, SUBLANES), :] = xi
        sr_ref[i] = xr[SUBLANES - 1:SUBLANES]
        si_ref[i] = xi[SUBLANES - 1:SUBLANES]
        return carry

    lax.fori_loop(0, nb, scan_block, 0)
    ys = _s5_readout(xr_scr[...], xi_scr[...], u, cr_ref, ci_ref, d_ref, wglu_ref, bglu_ref)
    ys_ref[...] = ys.reshape(nb, t_new, D_SSM).astype(ys_ref.dtype)

    tok = up[:, D_SSM:].reshape(nb, t_new, D_POOL)
    per_seq = POOL_HIST + t_new
    e = jnp.concatenate([hist_ref[...], tok], axis=1).reshape(nb * per_seq, D_POOL)
    sums = [c.reshape(nb, per_seq, D_POOL)[:, POOL_HIST:].reshape(nb * t_new, D_POOL)
            for c in _pool_window_sums(e)]
    pos = pos0 + lax.broadcasted_iota(jnp.int32, (nb, t_new, D_POOL), 1).reshape(nb * t_new, D_POOL)
    yq = _pool_mix(sums, up[:, D_SSM:], pos, wpool_ref, spool_ref)
    yq_ref[...] = yq.reshape(nb, t_new, D_POOL).astype(yq_ref.dtype)


def _seqmix_sample(up, x0r, x0i, hist, prm, layer, pos0, nb):
    bs, t_new, width = up.shape
    full = lambda a: _layer_spec(a, layer)
    seq = lambda r, w: pl.BlockSpec((nb, r, w), lambda i: (i, 0, 0))
    weights = _mix_weights(prm)
    return pl.pallas_call(
        functools.partial(_seqmix_sample_kernel, nb, pos0),
        grid=(bs // nb,),
        in_specs=[seq(t_new, width), seq(1, D_STATE), seq(1, D_STATE), seq(POOL_HIST, D_POOL)]
                 + [full(w) for w in weights],
        out_specs=[seq(t_new, D_SSM), seq(t_new, D_POOL), seq(1, D_STATE), seq(1, D_STATE)],
        out_shape=[jax.ShapeDtypeStruct((bs, t_new, D_SSM), BF16),
                   jax.ShapeDtypeStruct((bs, t_new, D_POOL), BF16),
                   jax.ShapeDtypeStruct((bs, 1, D_STATE), F32),
                   jax.ShapeDtypeStruct((bs, 1, D_STATE), F32)],
        scratch_shapes=[pltpu.VMEM((nb * t_new, D_STATE), F32), pltpu.VMEM((nb * t_new, D_STATE), F32)],
        compiler_params=_cparams(1),
        name="seqmix_sample",
    )(up, x0r, x0i, hist, *weights)


def _outproj_kernel(alpha, x_ref, att_ref, ys_ref, yq_ref, wo_ref, g_ref, b_ref, wr_ref, br_ref,
                    h_ref, gates_ref):
    y = alpha * x_ref[...]
    y = y + _dot(att_ref[...], wo_ref[0:D_ATT, :])
    y = y + _dot(ys_ref[...], wo_ref[D_ATT:D_ATT + D_SSM, :])
    y = y + _dot(yq_ref[...], wo_ref[D_ATT + D_SSM:, :])
    h = _layer_norm(y, g_ref[...], b_ref[...])
    h_ref[...] = h
    h_hi = h.astype(BF16)
    h_lo = (h - h_hi.astype(F32)).astype(BF16)
    both = _dot_nt(wr_ref[...], h_hi)
    logits = both[:N_EXPERTS] + both[N_EXPERTS:] + _dot_nt(wr_ref[0:N_EXPERTS, :], h_lo)
    aff = jax.nn.sigmoid(logits)
    biased = aff + br_ref[...]
    rows = [biased[e:e + 1, :] for e in range(N_EXPERTS)]

    def beats(a, ia, c, ic):
        return (a >= c) if ia < ic else (a > c)

    gscore = []
    for g in range(N_EXPERT_GROUPS):
        members = list(range(g * EXPERTS_PER_GROUP, (g + 1) * EXPERTS_PER_GROUP))
        total = jnp.zeros_like(rows[0])
        for e in members:
            rank = jnp.zeros_like(rows[0])
            for o in members:
                if o != e:
                    rank = rank + beats(rows[o], o, rows[e], e).astype(F32)
            total = total + jnp.where(rank < 2.0, rows[e], 0.0)
        gscore.append(total)
    in_group = []
    for g in range(N_EXPERT_GROUPS):
        lost = jnp.zeros_like(rows[0])
        for o in range(N_EXPERT_GROUPS):
            if o != g:
                lost = lost + beats(gscore[o], o, gscore[g], g).astype(F32)
        in_group.append(lost < 1.0)
    masked = [jnp.where(in_group[e // EXPERTS_PER_GROUP], rows[e], NEG_INF) for e in range(N_EXPERTS)]
    sel = []
    for e in range(N_EXPERTS):
        rank = jnp.zeros_like(rows[0])
        for o in range(N_EXPERTS):
            if o != e:
                rank = rank + beats(masked[o], o, masked[e], e).astype(F32)
        sel.append(jnp.where(rank < float(TOP_K), aff[e:e + 1, :], 0.0))
    denom = sel[0]
    for e in range(1, N_EXPERTS):
        denom = denom + sel[e]
    gates_ref[...] = jnp.concatenate(sel, axis=0) / denom


def _outproj(x, att, ys, yq, wo, ln_g, ln_b, wr_t, br, layer, alpha, tm):
    n, d = x.shape
    row = lambda w: pl.BlockSpec((tm, w), lambda i: (i, 0))
    return pl.pallas_call(
        functools.partial(_outproj_kernel, alpha),
        grid=(n // tm,),
        in_specs=[row(d), row(D_ATT), row(D_SSM), row(D_POOL), _layer_spec(wo, layer),
                  _layer_spec(ln_g, layer), _layer_spec(ln_b, layer), _shared_spec(wr_t), _shared_spec(br)],
        out_specs=[row(d), pl.BlockSpec((N_EXPERTS, tm), lambda i: (0, i))],
        out_shape=[jax.ShapeDtypeStruct((n, d), F32), jax.ShapeDtypeStruct((N_EXPERTS, n), F32)],
        compiler_params=_cparams(1),
        name="outproj_ln_router",
    )(x, att, ys, yq, wo, ln_g, ln_b, wr_t, br)


def _moe_kernel(alpha, d_expert, h_ref, gates_ref, wg_ref, wu_ref, wd_ref, g_ref, b_ref, o_ref,
                hb_ref, acc_ref):
    g = pl.program_id(1)

    @pl.when(g == 0)
    def _():
        hb_ref[...] = h_ref[...].astype(BF16)
        acc_ref[...] = jnp.zeros_like(acc_ref)

    hb = hb_ref[...]
    a = _dot(hb, wg_ref[...])
    b = _dot(hb, wu_ref[...])
    gates = gates_ref[0]
    parts = []
    for e in range(EXPERTS_PER_GROUP):
        cs = slice(e * d_expert, (e + 1) * d_expert)
        hid = jax.nn.silu(a[:, cs]) * b[:, cs] * gates[:, e:e + 1]
        parts.append(hid.astype(BF16))
    acc_ref[...] += _dot(jnp.concatenate(parts, axis=1), wd_ref[...])

    @pl.when(g == N_EXPERT_GROUPS - 1)
    def _():
        y = alpha * h_ref[...] + acc_ref[...]
        o_ref[...] = _layer_norm(y, g_ref[...], b_ref[...])


def _moe(h, gates_g, wg, wu, wd, ln_g, ln_b, layer, alpha, tm):
    n, d = h.shape
    gw = wg.shape[2] // N_EXPERT_GROUPS
    d_expert = gw // EXPERTS_PER_GROUP
    return pl.pallas_call(
        functools.partial(_moe_kernel, alpha, d_expert),
        grid=(n // tm, N_EXPERT_GROUPS),
        in_specs=[pl.BlockSpec((tm, d), lambda i, j: (i, 0)),
                  pl.BlockSpec((1, tm, EXPERTS_PER_GROUP), lambda i, j: (j, i, 0)),
                  pl.BlockSpec((None, d, gw), lambda i, j: (layer, 0, j)),
                  pl.BlockSpec((None, d, gw), lambda i, j: (layer, 0, j)),
                  pl.BlockSpec((None, gw, d), lambda i, j: (layer, j, 0)),
                  _layer_spec(ln_g, layer), _layer_spec(ln_b, layer)],
        out_specs=pl.BlockSpec((tm, d), lambda i, j: (i, 0)),
        out_shape=jax.ShapeDtypeStruct((n, d), F32),
        scratch_shapes=[pltpu.VMEM((tm, d), BF16), pltpu.VMEM((tm, d), F32)],
        compiler_params=_cparams(2),
        name="moe_ln",
    )(h, gates_g, wg, wu, wd, ln_g, ln_b)


def _post_block(x, att, ys, yq, lw, layer, alpha, tm):
    n = x.shape[0]
    h, gates_t = _outproj(x, att, ys, yq, lw["wo"], lw["ln1_g"], lw["ln1_b"], lw["wr_t"], lw["br"], layer,
                          alpha, tm)
    gates_g = gates_t.reshape(N_EXPERT_GROUPS, EXPERTS_PER_GROUP, n).transpose(0, 2, 1)
    return _moe(h, gates_g, lw["wg"], lw["wu"], lw["wd"], lw["ln2_g"], lw["ln2_b"], layer, alpha, tm)


def kernel(x_prompt, x_sample, cache_k, cache_v, cache_logf, page_table, state_ssm_re, state_ssm_im, state_pool, meta_tokens, ln0_g, ln0_b, w_in, b_forget, ssm_lam_re, ssm_lam_im, ssm_log_dt, ssm_b_re, ssm_b_im, ssm_c_re, ssm_c_im, ssm_d, w_glu, b_glu, w_pool, s_pool, w_out, ln1_g, ln1_b, w_router, b_router, w_gate, w_up, w_down, ln2_g, ln2_b):
    bp, seq, d_model = x_prompt.shape
    bs, t_new, _ = x_sample.shape
    depth = w_in.shape[0]
    n_pages, page = page_table.shape[1], cache_k.shape[2]
    past_len = n_pages * page
    length = N_META + seq
    lp = _round_up(length, LANES)
    tile = _pick_tile(lp, (640, 512, 384, 256, 128))
    alpha = (2 * depth) ** 0.25
    n_s = bs * t_new
    tm_s = _pick_tile(n_s, (512, 256, 128, 64, 32, 16, 8))
    assert t_new == SUBLANES and page == LANES and past_len >= POOL_BUF
    assert bs % DECODE_SEQS_PER_STEP == 0

    o1, o2, o3 = D_ATT, 2 * D_ATT, 3 * D_ATT
    o4 = o3 + N_HEADS
    row2 = lambda v: v.reshape(1, -1)

    kt_pages = cache_k.transpose(0, 1, 3, 4, 2).reshape(-1, depth, D_ATT, page)
    vt_pages = cache_v.transpose(0, 1, 3, 4, 2).reshape(-1, depth, D_ATT, page)
    lf_pages = cache_logf.transpose(0, 1, 3, 2)

    meta = jnp.broadcast_to(meta_tokens[None], (bp, N_META, d_model))
    xp = jnp.pad(x_prompt, ((0, 0), (N_META, lp - length), (0, 0)))
    xp = lax.dynamic_update_slice(xp, meta, (0, 0, 0))
    xs = x_sample.reshape(n_s, d_model)
    wr_hi = w_router.T.astype(BF16)
    wr_lo = (w_router.T - wr_hi.astype(F32)).astype(BF16)
    wr_t = jnp.concatenate([wr_hi, wr_lo], axis=0)
    br = b_router.reshape(N_EXPERTS, 1)
    nb_s = _pick_tile(bs, (16, 8, 4, 2, 1))

    outs_p = {k: [] for k in ("k", "v", "lf", "sr", "si", "pb")}
    outs_s = {k: [] for k in ("k", "v", "lf", "sr", "si", "pb")}
    rows3 = lambda v: v.reshape(depth, 1, -1)
    wb = w_in.astype(BF16)
    wq = wb[:, :, :o1]
    wup = wb[:, :, o4:]
    wt = jnp.concatenate([wb[:, :, o1:o3], wb[:, :, o3:o4]], axis=2).transpose(0, 2, 1)
    w_nat = jnp.concatenate([wb[:, :, :o3], wb[:, :, o4:]], axis=2)
    wf_t = wb[:, :, o3:o4].transpose(0, 2, 1)
    bf = b_forget.reshape(depth, N_HEADS, 1)

    apow_re, apow_im, bbr_t, bbi_t = _s5_params(ssm_lam_re, ssm_lam_im, ssm_log_dt, ssm_b_re, ssm_b_im)
    apr = apow_re.reshape(depth, SUBLANES, D_STATE)
    api = apow_im.reshape(depth, SUBLANES, D_STATE)
    ridx = jnp.arange(SUBLANES)[None, :, None]
    ak = []
    for k in (1, 2, 4):
        ak.append(jnp.where(ridx >= k, apr[:, k - 1][:, None], 0.0))
        ak.append(jnp.where(ridx >= k, api[:, k - 1][:, None], 0.0))
    ak += [apr, api]
    mix_w = dict(
        bbr=_block_diag(bbr_t).astype(BF16), bbi=_block_diag(bbi_t).astype(BF16),
        cr=_block_diag(ssm_c_re.transpose(0, 1, 3, 2)).astype(BF16),
        ci=_block_diag(ssm_c_im.transpose(0, 1, 3, 2)).astype(BF16),
        d=rows3(ssm_d), ak=jnp.stack(ak, axis=1), wglu=w_glu.astype(BF16), bglu=rows3(b_glu),
        wpool=_block_diag(w_pool).astype(BF16), spool=rows3(s_pool))
    d_expert = w_gate.shape[-1]
    lw = dict(
        wo=w_out.astype(BF16), ln1_g=rows3(ln1_g), ln1_b=rows3(ln1_b), wr_t=wr_t, br=br,
        wg=w_gate.transpose(0, 2, 1, 3).reshape(depth, d_model, N_EXPERTS * d_expert).astype(BF16),
        wu=w_up.transpose(0, 2, 1, 3).reshape(depth, d_model, N_EXPERTS * d_expert).astype(BF16),
        wd=w_down.reshape(depth, N_EXPERTS * d_expert, d_model).astype(BF16),
        ln2_g=rows3(ln2_g), ln2_b=rows3(ln2_b))

    for l in range(depth):
        res = _inproj_prompt(xp, row2(ln0_g), row2(ln0_b), wq, wup, wt, bf, l, l == 0, tile)
        if l == 0:
            xp, *res = res
        q, kt, vt, ka, vb, lft, u, p = res
        att = _attn_prompt(q, ka, vb, tile)
        ys, yq, sr, si = _seqmix(u, p, mix_w, l, length, tile)
        xp = _post_block(xp.reshape(bp * lp, d_model), att.reshape(bp * lp, D_ATT),
                         ys.reshape(bp * lp, D_SSM), yq.reshape(bp * lp, D_POOL), lw, l, alpha,
                         tile).reshape(bp, lp, d_model)
        outs_p["k"].append(kt[:, :, :length])
        outs_p["v"].append(vt[:, :, :length])
        outs_p["lf"].append(lft[:, :, :length])
        outs_p["sr"].append(sr.reshape(bp, N_SSM_GROUPS, SSM_STATE))
        outs_p["si"].append(si.reshape(bp, N_SSM_GROUPS, SSM_STATE))
        outs_p["pb"].append(p[:, length - POOL_BUF:length])

        res = _inproj_sample(xs, row2(ln0_g), row2(ln0_b), w_nat, wf_t, bf, l, l == 0, tm_s)
        if l == 0:
            xs, *res = res
        qkv, up, lfs_t = res
        lf_new = lfs_t.reshape(N_HEADS, bs, t_new).transpose(1, 0, 2)
        lf_new_pad = jnp.pad(lf_new, ((0, 0), (0, 0), (0, LANES - t_new)))
        att = _attn_decode(page_table, qkv.reshape(bs, t_new, 3 * D_ATT), lf_new_pad,
                           kt_pages, vt_pages, lf_pages, l)
        hist = jnp.concatenate([jnp.zeros((bs, 1, D_POOL), F32), state_pool[:, l]], axis=1)
        up3 = up.reshape(bs, t_new, D_SSM + D_POOL)
        p_s = up3[..., D_SSM:]
        ys, yq, sr, si = _seqmix_sample(up3, state_ssm_re[:, l].reshape(bs, 1, D_STATE),
                                        state_ssm_im[:, l].reshape(bs, 1, D_STATE), hist, mix_w, l, past_len,
                                        nb_s)
        xs = _post_block(xs, att.reshape(n_s, D_ATT), ys.reshape(n_s, D_SSM), yq.reshape(n_s, D_POOL),
                         lw, l, alpha, tm_s)
        qkv3 = qkv.reshape(bs, t_new, 3, N_HEADS, HEAD_DIM)
        outs_s["k"].append(qkv3[:, :, 1])
        outs_s["v"].append(qkv3[:, :, 2])
        outs_s["lf"].append(lf_new.transpose(0, 2, 1))
        outs_s["sr"].append(sr.reshape(bs, N_SSM_GROUPS, SSM_STATE))
        outs_s["si"].append(si.reshape(bs, N_SSM_GROUPS, SSM_STATE))
        outs_s["pb"].append(jnp.concatenate([state_pool[:, l], p_s], axis=1)[:, -POOL_BUF:])

    def heads_last(xs_t):
        a = jnp.stack(xs_t, axis=1)
        b_, dep, _, ln = a.shape
        return a.reshape(b_, dep, N_HEADS, HEAD_DIM, ln).transpose(0, 1, 4, 2, 3)

    y_prompt = xp[:, N_META:length]
    y_sample = xs.reshape(bs, t_new, d_model)
    return (y_prompt, y_sample,
            heads_last(outs_p["k"]), heads_last(outs_p["v"]),
            jnp.stack(outs_p["lf"], axis=1).transpose(0, 1, 3, 2),
            jnp.stack(outs_p["sr"], axis=1), jnp.stack(outs_p["si"], axis=1), jnp.stack(outs_p["pb"], axis=1),
            jnp.stack(outs_s["k"], axis=1), jnp.stack(outs_s["v"], axis=1), jnp.stack(outs_s["lf"], axis=1),
            jnp.stack(outs_s["sr"], axis=1), jnp.stack(outs_s["si"], axis=1), jnp.stack(outs_s["pb"], axis=1))
```

```python
import functools
import math

import jax
import jax.numpy as jnp
from jax import lax
from jax.experimental import pallas as pl
from jax.experimental.pallas import tpu as pltpu

F32 = jnp.float32
BF16 = jnp.bfloat16

N_META = 16
N_HEADS = 8
HEAD_DIM = 64
D_ATT = N_HEADS * HEAD_DIM
SSM_GROUP = 16
N_SSM_GROUPS = 16
SSM_STATE = 64
D_SSM = SSM_GROUP * N_SSM_GROUPS
D_STATE = N_SSM_GROUPS * SSM_STATE
POOL_WINDOWS = (2, 4, 8, 16)
POOL_GROUP = 64
D_POOL = POOL_GROUP * len(POOL_WINDOWS)
POOL_BUF = max(POOL_WINDOWS) - 1
POOL_HIST = POOL_BUF + 1
N_EXPERTS = 16
N_EXPERT_GROUPS = 4
EXPERTS_PER_GROUP = N_EXPERTS // N_EXPERT_GROUPS
TOP_K = 2
SCALE = HEAD_DIM ** -0.5
LOG2E = math.log2(math.e)
LN_EPS = 1e-5
NEG_INF = -1e30
HEADS_PER_STEP = 4
BIAS_PIECES = 3
SOFTMAX_ROWS = 32
DECODE_SEQS_PER_STEP = 2

LANES = 128
SUBLANES = 8
VMEM_LIMIT = 56 * 1024 * 1024


def _cparams(n_axes):
    return pltpu.CompilerParams(
        dimension_semantics=("arbitrary",) * n_axes, vmem_limit_bytes=VMEM_LIMIT)


def _round_up(x, m):
    return (x + m - 1) // m * m


def _shared_spec(a):
    return pl.BlockSpec(a.shape, lambda *_: (0,) * a.ndim)


def _layer_spec(a, layer):
    return pl.BlockSpec((None,) + a.shape[1:], lambda *_: (layer,) + (0,) * (a.ndim - 1))


def _pick_tile(n, candidates):
    for c in candidates:
        if n % c == 0:
            return c
    raise ValueError(f"no tile for {n}")


def _layer_norm(x, g, b):
    mu = jnp.mean(x, axis=-1, keepdims=True)
    xc = x - mu
    var = jnp.mean(xc * xc, axis=-1, keepdims=True)
    return xc * lax.rsqrt(var + LN_EPS) * g + b


def _log_sigmoid(x):
    return -(jnp.maximum(-x, 0.0) + jnp.log1p(jnp.exp(-jnp.abs(x))))


def _dot(a, b):
    return jnp.dot(a, b, preferred_element_type=F32)


def _dot_nt(a, b, precision=None):
    return lax.dot_general(a, b, (((1,), (1,)), ((), ())),
                           preferred_element_type=F32, precision=precision)


def _lane_cumsum(x):
    lane = lax.broadcasted_iota(jnp.int32, x.shape, 1)
    s = 1
    while s < LANES:
        x = x + jnp.where(lane >= s, pltpu.roll(x, s, axis=1), 0.0)
        s *= 2
    return x


def _s5_param_kernel(lr_ref, li_ref, ldt_ref, br_ref, bi_ref, apow_re, apow_im, bbr_ref, bbi_ref):
    lr = lr_ref[...]
    li = li_ref[...]
    dt = jnp.exp(ldt_ref[...])
    mag = jnp.exp(lr * dt)
    ab_re = mag * jnp.cos(li * dt)
    ab_im = mag * jnp.sin(li * dt)
    nr, ni = ab_re - 1.0, ab_im
    den = lr * lr + li * li
    gr = (nr * lr + ni * li) / den
    gi = (ni * lr - nr * li) / den
    br = br_ref[...]
    bi = bi_ref[...]
    bbr_ref[...] = gr[:, None, :] * br - gi[:, None, :] * bi
    bbi_ref[...] = gr[:, None, :] * bi + gi[:, None, :] * br
    pr, pi = ab_re, ab_im
    apow_re[0] = pr
    apow_im[0] = pi
    for k in range(1, SUBLANES):
        pr, pi = pr * ab_re - pi * ab_im, pr * ab_im + pi * ab_re
        apow_re[k] = pr
        apow_im[k] = pi


def _s5_params(lam_re, lam_im, log_dt, b_re, b_im):
    depth, g, p = lam_re.shape
    c = b_re.shape[-1]
    per_layer = lambda *dims: pl.BlockSpec((None,) + dims, lambda l: (l,) + (0,) * len(dims))
    out_shape = (jax.ShapeDtypeStruct((depth, SUBLANES, g, p), F32),
                 jax.ShapeDtypeStruct((depth, SUBLANES, g, p), F32),
                 jax.ShapeDtypeStruct((depth, g, c, p), F32), jax.ShapeDtypeStruct((depth, g, c, p), F32))
    return pl.pallas_call(
        _s5_param_kernel,
        grid=(depth,),
        in_specs=[per_layer(g, p), per_layer(g, p), per_layer(g, 1), per_layer(g, c, p), per_layer(g, c, p)],
        out_specs=[per_layer(SUBLANES, g, p), per_layer(SUBLANES, g, p), per_layer(g, c, p),
                   per_layer(g, c, p)],
        out_shape=out_shape,
        compiler_params=_cparams(1),
        name="s5_params",
    )(lam_re, lam_im, log_dt.reshape(depth, g, 1), b_re.transpose(0, 1, 3, 2), b_im.transpose(0, 1, 3, 2))


def _block_diag(w):
    dep, g, a, b = w.shape
    eye = jnp.eye(g, dtype=w.dtype)
    return (w[:, :, :, None, :] * eye[None, :, None, :, None]).reshape(dep, g * a, g * b)


def _inproj_prompt_kernel(apply_ln, tm, x_ref, g_ref, b_ref, wq_ref, wup_ref, wt_ref, bf_ref, *outs):
    if apply_ln:
        xn_ref, q_ref, kt_ref, vt_ref, ka_ref, vb_ref, lft_ref, u_ref, p_ref, carry_ref, ck_scr = outs
    else:
        q_ref, kt_ref, vt_ref, ka_ref, vb_ref, lft_ref, u_ref, p_ref, carry_ref, ck_scr = outs
    t = pl.program_id(1)
    x = x_ref[0]
    if apply_ln:
        x = _layer_norm(x, g_ref[...], b_ref[...])
        xn_ref[0] = x
    xb = x.astype(BF16)
    q = _dot(xb, wq_ref[...]) * (SCALE * LOG2E)
    lane = lax.broadcasted_iota(jnp.int32, (tm, LANES), 1)
    ones = jnp.where(lane < HEAD_DIM + BIAS_PIECES, 1.0, 0.0)
    for h in range(N_HEADS):
        pair = q[:, (h // 2) * LANES:(h // 2 + 1) * LANES]
        if h % 2:
            pair = pltpu.roll(pair, HEAD_DIM, axis=1)
        q_ref[0, h] = jnp.where(lane < HEAD_DIM, pair, ones).astype(BF16)
    up = _dot(xb, wup_ref[...])
    u_ref[0] = up[:, :D_SSM]
    p_ref[0] = up[:, D_SSM:]
    zt = _dot_nt(wt_ref[...], xb)
    kt = zt[:D_ATT]
    vt = zt[D_ATT:2 * D_ATT]
    kt_ref[0] = kt
    vt_ref[0] = vt
    ones_row = lax.broadcasted_iota(jnp.int32, (HEAD_DIM, tm), 0) == 0
    for h in range(N_HEADS):
        vb_ref[0, h, 0:HEAD_DIM, :] = vt[h * HEAD_DIM:(h + 1) * HEAD_DIM].astype(BF16)
        vb_ref[0, h, HEAD_DIM:, :] = jnp.where(ones_row, 1.0, 0.0).astype(BF16)
    lf = _log_sigmoid(zt[2 * D_ATT:] + bf_ref[...])
    lft_ref[0] = lf

    @pl.when(t == 0)
    def _():
        carry_ref[...] = jnp.zeros_like(carry_ref)

    carry = carry_ref[...]
    for c in range(tm // LANES):
        blk = _lane_cumsum(lf[:, c * LANES:(c + 1) * LANES]) + carry
        ck_scr[:, c * LANES:(c + 1) * LANES] = blk
        carry = jnp.broadcast_to(blk[:, LANES - 1:LANES], carry.shape)
    carry_ref[...] = carry

    piece_row = lax.broadcasted_iota(jnp.int32, (HEAD_DIM, tm), 0)
    for h in range(N_HEADS):
        ka_ref[0, h, 0:HEAD_DIM, :] = kt[h * HEAD_DIM:(h + 1) * HEAD_DIM].astype(BF16)
        rest = ck_scr[h:h + 1, :] * (-LOG2E)
        aug = jnp.zeros((HEAD_DIM, tm), F32)
        for i in range(BIAS_PIECES):
            piece = rest.astype(BF16).astype(F32)
            aug = jnp.where(piece_row == i, piece, aug)
            rest = rest - piece
        ka_ref[0, h, HEAD_DIM:, :] = aug.astype(BF16)


def _inproj_prompt(x, ln_g, ln_b, wq, wup, wt, bf, layer, apply_ln, tm):
    b, lp, d = x.shape
    nt = lp // tm
    out_shape, out_specs = [], []
    if apply_ln:
        out_shape.append(jax.ShapeDtypeStruct((b, lp, d), F32))
        out_specs.append(pl.BlockSpec((1, tm, d), lambda i, j: (i, j, 0)))
    out_shape += [
        jax.ShapeDtypeStruct((b, N_HEADS, lp, LANES), BF16),
        jax.ShapeDtypeStruct((b, D_ATT, lp), F32),
        jax.ShapeDtypeStruct((b, D_ATT, lp), F32),
        jax.ShapeDtypeStruct((b, N_HEADS, LANES, lp), BF16),
        jax.ShapeDtypeStruct((b, N_HEADS, LANES, lp), BF16),
        jax.ShapeDtypeStruct((b, N_HEADS, lp), F32),
        jax.ShapeDtypeStruct((b, lp, D_SSM), F32),
        jax.ShapeDtypeStruct((b, lp, D_POOL), F32),
    ]
    out_specs += [
        pl.BlockSpec((1, N_HEADS, tm, LANES), lambda i, j: (i, 0, j, 0)),
        pl.BlockSpec((1, D_ATT, tm), lambda i, j: (i, 0, j)),
        pl.BlockSpec((1, D_ATT, tm), lambda i, j: (i, 0, j)),
        pl.BlockSpec((1, N_HEADS, LANES, tm), lambda i, j: (i, 0, 0, j)),
        pl.BlockSpec((1, N_HEADS, LANES, tm), lambda i, j: (i, 0, 0, j)),
        pl.BlockSpec((1, N_HEADS, tm), lambda i, j: (i, 0, j)),
        pl.BlockSpec((1, tm, D_SSM), lambda i, j: (i, j, 0)),
        pl.BlockSpec((1, tm, D_POOL), lambda i, j: (i, j, 0)),
    ]
    return pl.pallas_call(
        functools.partial(_inproj_prompt_kernel, apply_ln, tm),
        grid=(b, nt),
        in_specs=[pl.BlockSpec((1, tm, d), lambda i, j: (i, j, 0)),
                  _shared_spec(ln_g), _shared_spec(ln_b), _layer_spec(wq, layer), _layer_spec(wup, layer),
                  _layer_spec(wt, layer), _layer_spec(bf, layer)],
        out_specs=out_specs,
        out_shape=out_shape,
        scratch_shapes=[pltpu.VMEM((N_HEADS, LANES), F32), pltpu.VMEM((N_HEADS, tm), F32)],
        compiler_params=_cparams(2),
        name="inproj_prompt",
    )(x, ln_g, ln_b, wq, wup, wt, bf)


def _inproj_sample_kernel(apply_ln, x_ref, g_ref, b_ref, w_ref, wf_ref, bf_ref, *outs):
    if apply_ln:
        xn_ref, qkv_ref, up_ref, lft_ref = outs
    else:
        qkv_ref, up_ref, lft_ref = outs
    x = x_ref[...]
    if apply_ln:
        x = _layer_norm(x, g_ref[...], b_ref[...])
        xn_ref[...] = x
    xb = x.astype(BF16)
    z = _dot(xb, w_ref[...])
    qkv_ref[...] = z[:, :3 * D_ATT]
    up_ref[...] = z[:, 3 * D_ATT:]
    lft_ref[...] = _log_sigmoid(_dot_nt(wf_ref[...], xb) + bf_ref[...])


def _inproj_sample(x, ln_g, ln_b, w, wf, bf, layer, apply_ln, tm):
    n, d = x.shape
    out_shape, out_specs = [], []
    if apply_ln:
        out_shape.append(jax.ShapeDtypeStruct((n, d), F32))
        out_specs.append(pl.BlockSpec((tm, d), lambda i: (i, 0)))
    out_shape += [jax.ShapeDtypeStruct((n, 3 * D_ATT), F32),
                  jax.ShapeDtypeStruct((n, D_SSM + D_POOL), F32),
                  jax.ShapeDtypeStruct((N_HEADS, n), F32)]
    out_specs += [pl.BlockSpec((tm, 3 * D_ATT), lambda i: (i, 0)),
                  pl.BlockSpec((tm, D_SSM + D_POOL), lambda i: (i, 0)),
                  pl.BlockSpec((N_HEADS, tm), lambda i: (0, i))]
    return pl.pallas_call(
        functools.partial(_inproj_sample_kernel, apply_ln),
        grid=(n // tm,),
        in_specs=[pl.BlockSpec((tm, d), lambda i: (i, 0)),
                  _shared_spec(ln_g), _shared_spec(ln_b), _layer_spec(w, layer), _layer_spec(wf, layer),
                  _layer_spec(bf, layer)],
        out_specs=out_specs,
        out_shape=out_shape,
        compiler_params=_cparams(1),
        name="inproj_sample",
    )(x, ln_g, ln_b, w, wf, bf)


def _attn_prompt_kernel(tq, q_ref, ka_ref, vb_ref, o_ref,
                        s_a, s_b, p_a, p_b, al_a, al_b, m_scr, acc_scr):
    qi = pl.program_id(2)
    rc = SOFTMAX_ROWS
    m_scr[...] = jnp.full(m_scr.shape, NEG_INF, F32)
    acc_scr[...] = jnp.zeros(acc_scr.shape, F32)

    def scores(k, s_buf, j):
        k0 = pl.multiple_of(k * tq, LANES)
        s_buf[j] = _dot(q_ref[0, j], ka_ref[0, j, :, pl.ds(k0, tq)])

    def chunk(s_buf, j, c, masked):
        rows = slice(c * rc, (c + 1) * rc)
        sc = s_buf[j, rows, :]
        if masked:
            row = lax.broadcasted_iota(jnp.int32, (rc, tq), 0)
            col = lax.broadcasted_iota(jnp.int32, (rc, tq), 1)
            sc = jnp.where(col <= row + c * rc, sc, NEG_INF)
        return rows, sc

    def row_max(s_buf, al_buf, j, masked):
        for c in range(tq // rc):
            rows, sc = chunk(s_buf, j, c, masked)
            m_old = m_scr[j, rows, :]
            m_new = jnp.maximum(m_old, jnp.max(sc, axis=1, keepdims=True))
            al_buf[j, rows, :] = jnp.exp2(m_old - m_new)
            m_scr[j, rows, :] = m_new

    def probs(s_buf, p_buf, j, masked):
        for c in range(tq // rc):
            rows, sc = chunk(s_buf, j, c, masked)
            x = sc - jnp.concatenate([m_scr[j, rows, :]] * (tq // LANES), axis=1)
            p_buf[j, rows, :] = jnp.exp2(x.astype(BF16))

    def values(k, p_buf, al_buf, j):
        k0 = pl.multiple_of(k * tq, LANES)
        acc_scr[j] = al_buf[j] * acc_scr[j] + _dot_nt(p_buf[j], vb_ref[0, j, :, pl.ds(k0, tq)])

    for j in range(HEADS_PER_STEP):
        scores(0, s_a, j)
    p_b[...] = jnp.zeros(p_b.shape, BF16)
    al_b[...] = jnp.ones(al_b.shape, F32)

    def stage(k, s_cur, s_nxt, p_cur, p_prev, al_cur, al_prev):
        k_prev = jnp.maximum(k - 1, 0)
        for j in range(HEADS_PER_STEP):
            scores(k + 1, s_nxt, j)
        for j in range(HEADS_PER_STEP):
            row_max(s_cur, al_cur, j, False)
        for j in range(HEADS_PER_STEP):
            probs(s_cur, p_cur, j, False)
        for j in range(HEADS_PER_STEP):
            values(k_prev, p_prev, al_prev, j)

    def body(k, carry):
        @pl.when(k % 2 == 0)
        def _():
            stage(k, s_a, s_b, p_a, p_b, al_a, al_b)

        @pl.when(k % 2 == 1)
        def _():
            stage(k, s_b, s_a, p_b, p_a, al_b, al_a)

        return carry

    lax.fori_loop(0, qi, body, 0)

    def drain(s_cur, p_cur, p_prev, al_cur, al_prev):
        k_prev = jnp.maximum(qi - 1, 0)
        for j in range(HEADS_PER_STEP):
            row_max(s_cur, al_cur, j, True)
        for j in range(HEADS_PER_STEP):
            values(k_prev, p_prev, al_prev, j)
            probs(s_cur, p_cur, j, True)
        for j in range(HEADS_PER_STEP):
            values(qi, p_cur, al_cur, j)

    @pl.when(qi % 2 == 0)
    def _():
        drain(s_a, p_a, p_b, al_a, al_b)

    @pl.when(qi % 2 == 1)
    def _():
        drain(s_b, p_b, p_a, al_b, al_a)

    outs = []
    for j in range(HEADS_PER_STEP):
        acc = acc_scr[j]
        outs.append(acc[:, 0:HEAD_DIM] / acc[:, HEAD_DIM:HEAD_DIM + 1])
    o_ref[0] = jnp.concatenate(outs, axis=1).astype(o_ref.dtype)


def _attn_prompt(q, ka, vb, tq):
    b, _, lp, _ = q.shape
    hp = N_HEADS // HEADS_PER_STEP
    rows = HEADS_PER_STEP * HEAD_DIM
    return pl.pallas_call(
        functools.partial(_attn_prompt_kernel, tq),
        grid=(b, hp, lp // tq),
        in_specs=[pl.BlockSpec((1, HEADS_PER_STEP, tq, LANES), lambda i, h, j: (i, h, j, 0)),
                  pl.BlockSpec((1, HEADS_PER_STEP, LANES, lp), lambda i, h, j: (i, h, 0, 0),
                               pipeline_mode=pl.Buffered(1)),
                  pl.BlockSpec((1, HEADS_PER_STEP, LANES, lp), lambda i, h, j: (i, h, 0, 0),
                               pipeline_mode=pl.Buffered(1))],
        out_specs=pl.BlockSpec((1, tq, rows), lambda i, h, j: (i, j, h)),
        out_shape=jax.ShapeDtypeStruct((b, lp, D_ATT), BF16),
        scratch_shapes=[pltpu.VMEM((HEADS_PER_STEP, tq, tq), F32), pltpu.VMEM((HEADS_PER_STEP, tq, tq), F32),
                        pltpu.VMEM((HEADS_PER_STEP, tq, tq), BF16), pltpu.VMEM((HEADS_PER_STEP, tq, tq), BF16),
                        pltpu.VMEM((HEADS_PER_STEP, tq, LANES), F32), pltpu.VMEM((HEADS_PER_STEP, tq, LANES), F32),
                        pltpu.VMEM((HEADS_PER_STEP, tq, LANES), F32),
                        pltpu.VMEM((HEADS_PER_STEP, tq, LANES), F32)],
        compiler_params=_cparams(3),
        name="attn_prompt",
    )(q, ka, vb)


def _attn_decode_kernel(n_pages, t_new, layer, n_steps, pt_ref, qkv_ref, lfn_ref, lf_ref, kt_hbm, vt_hbm,
                        o_ref, kbuf, vbuf, ksem, vsem):
    step = pl.program_id(0)
    slot = lax.rem(step, 2)
    n_cp = DECODE_SEQS_PER_STEP * n_pages

    def page_copies(s, slot_):
        cps = []
        for i in range(n_cp):
            page_id = pt_ref[s * n_cp + i]
            cps.append(pltpu.make_async_copy(kt_hbm.at[page_id, layer], kbuf.at[slot_, i], ksem.at[slot_]))
            cps.append(pltpu.make_async_copy(vt_hbm.at[page_id, layer], vbuf.at[slot_, i], vsem.at[slot_]))
        return cps

    @pl.when(step == 0)
    def _():
        for cp in page_copies(0, 0):
            cp.start()

    @pl.when(step + 1 < n_steps)
    def _():
        for cp in page_copies(step + 1, 1 - slot):
            cp.start()

    for cp in page_copies(step, slot):
        cp.wait()

    rows = t_new * N_HEADS
    head_of_row = lax.broadcasted_iota(jnp.int32, (N_HEADS, D_ATT), 0)
    head_of_col = lax.broadcasted_iota(jnp.int32, (N_HEADS, D_ATT), 1) // HEAD_DIM
    hmask = head_of_row == head_of_col
    omask = jnp.concatenate([hmask] * t_new, axis=0)
    key_t = lax.broadcasted_iota(jnp.int32, (rows, LANES), 1)
    qry_t = lax.broadcasted_iota(jnp.int32, (rows, LANES), 0) // N_HEADS
    pad = jnp.zeros((LANES - t_new, D_ATT), F32)

    def expand(c):
        return jnp.concatenate([c] * t_new, axis=0)

    for sq in range(DECODE_SEQS_PER_STEP):
        base = n_pages * sq
        pt0 = step * n_cp + base
        qkv = qkv_ref[sq]
        q = qkv[:, :D_ATT] * SCALE
        k_new = qkv[:, D_ATT:2 * D_ATT]
        v_new = qkv[:, 2 * D_ATT:]
        qbd = jnp.concatenate(
            [jnp.where(hmask, jnp.broadcast_to(q[t:t + 1], (N_HEADS, D_ATT)), 0.0) for t in range(t_new)],
            axis=0).astype(BF16)

        local = _lane_cumsum(jnp.concatenate([lf_ref[pt_ref[pt0 + j], 0] for j in range(n_pages)]
                                             + [lfn_ref[sq]], axis=0))
        off = jnp.zeros((N_HEADS, LANES), F32)
        s_blocks = []
        for j in range(n_pages):
            ck = local[j * N_HEADS:(j + 1) * N_HEADS] + off
            off = jnp.broadcast_to(ck[:, LANES - 1:LANES], off.shape)
            s = _dot(qbd, kbuf[slot, base + j].astype(BF16))
            s_blocks.append(s - expand(ck))
        k_pad = jnp.concatenate([k_new, pad], axis=0).astype(BF16)
        v_pad = jnp.concatenate([v_new, pad], axis=0).astype(BF16)
        ck_new = local[n_pages * N_HEADS:] + off
        s_new = _dot_nt(qbd, k_pad) - expand(ck_new)
        s_blocks.append(jnp.where(key_t <= qry_t, s_new, NEG_INF))

        m = s_blocks[0]
        for s in s_blocks[1:]:
            m = jnp.maximum(m, s)
        m = m.max(axis=1, keepdims=True)
        l = jnp.zeros((rows, LANES), F32)
        acc = jnp.zeros((rows, D_ATT), F32)
        for j, s in enumerate(s_blocks):
            p = jnp.exp(s - m)
            l = l + p
            if j < n_pages:
                acc = acc + _dot_nt(p.astype(BF16), vbuf[slot, base + j].astype(BF16))
            else:
                acc = acc + _dot(p.astype(BF16), v_pad)
        acc = acc / l.sum(axis=1, keepdims=True)
        acc = jnp.where(omask, acc, 0.0)
        o_ref[sq] = acc.reshape(t_new, N_HEADS, D_ATT).sum(axis=1).astype(o_ref.dtype)


def _attn_decode(page_table, qkv, lf_new_t, kt_pages, vt_pages, lf_pages, layer):
    bs, t_new, _ = qkv.shape
    n_pages = page_table.shape[1]
    page = kt_pages.shape[-1]
    sps = DECODE_SEQS_PER_STEP
    n_steps = bs // sps
    pt_flat = page_table.reshape(-1)
    in_specs = [pl.BlockSpec((sps, t_new, 3 * D_ATT), lambda i, pt: (i, 0, 0)),
                pl.BlockSpec((sps, N_HEADS, LANES), lambda i, pt: (i, 0, 0)),
                pl.BlockSpec((lf_pages.shape[0], 1, N_HEADS, page), lambda i, pt: (0, layer, 0, 0),
                             pipeline_mode=pl.Buffered(1)),
                pl.BlockSpec(memory_space=pl.ANY),
                pl.BlockSpec(memory_space=pl.ANY)]
    page_buf = pltpu.VMEM((2, sps * n_pages, D_ATT, page), F32)
    grid_spec = pltpu.PrefetchScalarGridSpec(
        num_scalar_prefetch=1, grid=(n_steps,), in_specs=in_specs,
        out_specs=pl.BlockSpec((sps, t_new, D_ATT), lambda i, pt: (i, 0, 0)),
        scratch_shapes=[page_buf, page_buf, pltpu.SemaphoreType.DMA((2,)), pltpu.SemaphoreType.DMA((2,))])
    return pl.pallas_call(
        functools.partial(_attn_decode_kernel, n_pages, t_new, layer, n_steps),
        grid_spec=grid_spec,
        out_shape=jax.ShapeDtypeStruct((bs, t_new, D_ATT), BF16),
        compiler_params=_cparams(1),
        name="attn_decode",
    )(pt_flat, qkv, lf_new_t, lf_pages, kt_pages, vt_pages)


def _s5_scan_rows(br, bi, cr, ci, ak_ref):
    for idx, k in enumerate((1, 2, 4)):
        akr = ak_ref[2 * idx]
        aki = ak_ref[2 * idx + 1]
        sr = pltpu.roll(br, k, axis=0)
        si = pltpu.roll(bi, k, axis=0)
        br, bi = br + akr * sr - aki * si, bi + akr * si + aki * sr
    pr = ak_ref[6]
    pi = ak_ref[7]
    return br + pr * cr - pi * ci, bi + pr * ci + pi * cr


def _s5_readout(xr, xi, u, cr_ref, ci_ref, d_ref, wglu_ref, bglu_ref):
    y = _dot(xr.astype(BF16), cr_ref[...]) - _dot(xi.astype(BF16), ci_ref[...]) + d_ref[...] * u
    y = jax.nn.gelu(y)
    return y * jax.nn.sigmoid(_dot(y.astype(BF16), wglu_ref[...]) + bglu_ref[...])


def _pool_window_sums(e):
    c2 = e + pltpu.roll(e, 1, axis=0)
    c4 = c2 + pltpu.roll(c2, 2, axis=0)
    c8 = c4 + pltpu.roll(c4, 4, axis=0)
    c16 = c8 + pltpu.roll(c8, 8, axis=0)
    return c2, c4, c8, c16


def _pool_mix(sums, tok, pos, wpool_ref, spool_ref):
    c2, c4, c8, c16 = sums
    grp = lax.broadcasted_iota(jnp.int32, tok.shape, 1) // POOL_GROUP
    win = jnp.where(grp == 0, c2, jnp.where(grp == 1, c4, jnp.where(grp == 2, c8, c16)))
    wsize = jnp.where(grp == 0, 2, jnp.where(grp == 1, 4, jnp.where(grp == 2, 8, 16)))
    cnt = jnp.minimum(wsize, pos + 1).astype(F32)
    mm = win / cnt - tok
    return _dot(mm.astype(BF16), wpool_ref[...]) * spool_ref[...]


def _seqmix_kernel(tt, snap_tile, snap_row,
                   u_ref, p_ref, bbr_ref, bbi_ref, cr_ref, ci_ref, d_ref, ak_ref, wglu_ref, bglu_ref,
                   wpool_ref, spool_ref,
                   ys_ref, yq_ref, sr_ref, si_ref,
                   xr_scr, xi_scr, car_r, car_i, ext_scr):
    t = pl.program_id(1)

    @pl.when(t == 0)
    def _():
        car_r[...] = jnp.zeros_like(car_r)
        car_i[...] = jnp.zeros_like(car_i)
        ext_scr[0:POOL_HIST, :] = jnp.zeros((POOL_HIST, D_POOL), F32)

    u = u_ref[0]
    ub = u.astype(BF16)
    xr_scr[...] = _dot(ub, bbr_ref[...])
    xi_scr[...] = _dot(ub, bbi_ref[...])

    def scan_block(i, carry):
        cr, ci = carry
        r0 = pl.multiple_of(i * SUBLANES, SUBLANES)
        xr, xi = _s5_scan_rows(xr_scr[pl.ds(r0, SUBLANES), :], xi_scr[pl.ds(r0, SUBLANES), :], cr, ci, ak_ref)
        xr_scr[pl.ds(r0, SUBLANES), :] = xr
        xi_scr[pl.ds(r0, SUBLANES), :] = xi
        return xr[SUBLANES - 1:SUBLANES], xi[SUBLANES - 1:SUBLANES]

    cr, ci = lax.fori_loop(0, tt // SUBLANES, scan_block, (car_r[...], car_i[...]))
    car_r[...] = cr
    car_i[...] = ci

    @pl.when(t == snap_tile)
    def _():
        sr_ref[0] = xr_scr[snap_row:snap_row + 1, :]
        si_ref[0] = xi_scr[snap_row:snap_row + 1, :]

    ys_ref[0] = _s5_readout(xr_scr[...], xi_scr[...], u, cr_ref, ci_ref, d_ref, wglu_ref,
                            bglu_ref).astype(ys_ref.dtype)

    ext_scr[POOL_HIST:POOL_HIST + tt, :] = p_ref[0]
    e = ext_scr[...]
    sums = [c[POOL_HIST:] for c in _pool_window_sums(e)]
    pos = t * tt + lax.broadcasted_iota(jnp.int32, (tt, D_POOL), 0)
    yq_ref[0] = _pool_mix(sums, e[POOL_HIST:], pos, wpool_ref, spool_ref).astype(yq_ref.dtype)
    ext_scr[0:POOL_HIST, :] = e[tt:tt + POOL_HIST]


def _mix_weights(prm):
    return (prm["bbr"], prm["bbi"], prm["cr"], prm["ci"], prm["d"], prm["ak"], prm["wglu"],
            prm["bglu"], prm["wpool"], prm["spool"])


def _seqmix(u, p, prm, layer, length, tt):
    b, lp, _ = u.shape
    nt = lp // tt
    snap_tile, snap_row = (length - 1) // tt, (length - 1) % tt
    full = lambda a: _layer_spec(a, layer)
    weights = _mix_weights(prm)
    return pl.pallas_call(
        functools.partial(_seqmix_kernel, tt, snap_tile, snap_row),
        grid=(b, nt),
        in_specs=[pl.BlockSpec((1, tt, D_SSM), lambda i, j: (i, j, 0)),
                  pl.BlockSpec((1, tt, D_POOL), lambda i, j: (i, j, 0))]
                 + [full(w) for w in weights],
        out_specs=[pl.BlockSpec((1, tt, D_SSM), lambda i, j: (i, j, 0)),
                   pl.BlockSpec((1, tt, D_POOL), lambda i, j: (i, j, 0)),
                   pl.BlockSpec((1, 1, D_STATE), lambda i, j: (i, 0, 0)),
                   pl.BlockSpec((1, 1, D_STATE), lambda i, j: (i, 0, 0))],
        out_shape=[jax.ShapeDtypeStruct((b, lp, D_SSM), BF16),
                   jax.ShapeDtypeStruct((b, lp, D_POOL), BF16),
                   jax.ShapeDtypeStruct((b, 1, D_STATE), F32),
                   jax.ShapeDtypeStruct((b, 1, D_STATE), F32)],
        scratch_shapes=[pltpu.VMEM((tt, D_STATE), F32), pltpu.VMEM((tt, D_STATE), F32),
                        pltpu.VMEM((1, D_STATE), F32), pltpu.VMEM((1, D_STATE), F32),
                        pltpu.VMEM((POOL_HIST + tt, D_POOL), F32)],
        compiler_params=_cparams(2),
        name="seqmix",
    )(u, p, *weights)


def _seqmix_sample_kernel(nb, pos0, up_ref, x0r_ref, x0i_ref, hist_ref,
                          bbr_ref, bbi_ref, cr_ref, ci_ref, d_ref, ak_ref, wglu_ref, bglu_ref,
                          wpool_ref, spool_ref,
                          ys_ref, yq_ref, sr_ref, si_ref, xr_scr, xi_scr):
    t_new = SUBLANES
    up = up_ref[...].reshape(nb * t_new, D_SSM + D_POOL)
    u = up[:, :D_SSM]
    ub = u.astype(BF16)
    xr_scr[...] = _dot(ub, bbr_ref[...])
    xi_scr[...] = _dot(ub, bbi_ref[...])

    def scan_block(i, carry):
        r0 = pl.multiple_of(i * SUBLANES, SUBLANES)
        xr, xi = _s5_scan_rows(xr_scr[pl.ds(r0, SUBLANES), :], xi_scr[pl.ds(r0, SUBLANES), :],
                               x0r_ref[i], x0i_ref[i], ak_ref)
        xr_scr[pl.ds(r0, SUBLANES), :] = xr
        xi_scr[pl.ds(r0, SUBLANES), :] = xi
        sr_ref[i] = xr[SUBLANES - 1:SUBLANES]
        si_ref[i] = xi[SUBLANES - 1:SUBLANES]
        return carry

    lax.fori_loop(0, nb, scan_block, 0)
    ys = _s5_readout(xr_scr[...], xi_scr[...], u, cr_ref, ci_ref, d_ref, wglu_ref, bglu_ref)
    ys_ref[...] = ys.reshape(nb, t_new, D_SSM).astype(ys_ref.dtype)

    tok = up[:, D_SSM:].reshape(nb, t_new, D_POOL)
    per_seq = POOL_HIST + t_new
    e = jnp.concatenate([hist_ref[...], tok], axis=1).reshape(nb * per_seq, D_POOL)
    sums = [c.reshape(nb, per_seq, D_POOL)[:, POOL_HIST:].reshape(nb * t_new, D_POOL)
            for c in _pool_window_sums(e)]
    pos = pos0 + lax.broadcasted_iota(jnp.int32, (nb, t_new, D_POOL), 1).reshape(nb * t_new, D_POOL)
    yq = _pool_mix(sums, up[:, D_SSM:], pos, wpool_ref, spool_ref)
    yq_ref[...] = yq.reshape(nb, t_new, D_POOL).astype(yq_ref.dtype)


def _seqmix_sample(up, x0r, x0i, hist, prm, layer, pos0, nb):
    bs, t_new, width = up.shape
    full = lambda a: _layer_spec(a, layer)
    seq = lambda r, w: pl.BlockSpec((nb, r, w), lambda i: (i, 0, 0))
    weights = _mix_weights(prm)
    return pl.pallas_call(
        functools.partial(_seqmix_sample_kernel, nb, pos0),
        grid=(bs // nb,),
        in_specs=[seq(t_new, width), seq(1, D_STATE), seq(1, D_STATE), seq(POOL_HIST, D_POOL)]
                 + [full(w) for w in weights],
        out_specs=[seq(t_new, D_SSM), seq(t_new, D_POOL), seq(1, D_STATE), seq(1, D_STATE)],
        out_shape=[jax.ShapeDtypeStruct((bs, t_new, D_SSM), BF16),
                   jax.ShapeDtypeStruct((bs, t_new, D_POOL), BF16),
                   jax.ShapeDtypeStruct((bs, 1, D_STATE), F32),
                   jax.ShapeDtypeStruct((bs, 1, D_STATE), F32)],
        scratch_shapes=[pltpu.VMEM((nb * t_new, D_STATE), F32), pltpu.VMEM((nb * t_new, D_STATE), F32)],
        compiler_params=_cparams(1),
        name="seqmix_sample",
    )(up, x0r, x0i, hist, *weights)


def _outproj_kernel(alpha, x_ref, att_ref, ys_ref, yq_ref, wo_ref, g_ref, b_ref, wr_ref, br_ref,
                    h_ref, gates_ref):
    y = alpha * x_ref[...]
    y = y + _dot(att_ref[...], wo_ref[0:D_ATT, :])
    y = y + _dot(ys_ref[...], wo_ref[D_ATT:D_ATT + D_SSM, :])
    y = y + _dot(yq_ref[...], wo_ref[D_ATT + D_SSM:, :])
    h = _layer_norm(y, g_ref[...], b_ref[...])
    h_ref[...] = h
    h_hi = h.astype(BF16)
    h_lo = (h - h_hi.astype(F32)).astype(BF16)
    both = _dot_nt(wr_ref[...], h_hi)
    logits = both[:N_EXPERTS] + both[N_EXPERTS:] + _dot_nt(wr_ref[0:N_EXPERTS, :], h_lo)
    aff = jax.nn.sigmoid(logits)
    biased = aff + br_ref[...]
    rows = [biased[e:e + 1, :] for e in range(N_EXPERTS)]

    def beats(a, ia, c, ic):
        return (a >= c) if ia < ic else (a > c)

    gscore = []
    for g in range(N_EXPERT_GROUPS):
        members = list(range(g * EXPERTS_PER_GROUP, (g + 1) * EXPERTS_PER_GROUP))
        total = jnp.zeros_like(rows[0])
        for e in members:
            rank = jnp.zeros_like(rows[0])
            for o in members:
                if o != e:
                    rank = rank + beats(rows[o], o, rows[e], e).astype(F32)
            total = total + jnp.where(rank < 2.0, rows[e], 0.0)
        gscore.append(total)
    in_group = []
    for g in range(N_EXPERT_GROUPS):
        lost = jnp.zeros_like(rows[0])
        for o in range(N_EXPERT_GROUPS):
            if o != g:
                lost = lost + beats(gscore[o], o, gscore[g], g).astype(F32)
        in_group.append(lost < 1.0)
    masked = [jnp.where(in_group[e // EXPERTS_PER_GROUP], rows[e], NEG_INF) for e in range(N_EXPERTS)]
    sel = []
    for e in range(N_EXPERTS):
        rank = jnp.zeros_like(rows[0])
        for o in range(N_EXPERTS):
            if o != e:
                rank = rank + beats(masked[o], o, masked[e], e).astype(F32)
        sel.append(jnp.where(rank < float(TOP_K), aff[e:e + 1, :], 0.0))
    denom = sel[0]
    for e in range(1, N_EXPERTS):
        denom = denom + sel[e]
    gates_ref[...] = jnp.concatenate(sel, axis=0) / denom


def _outproj(x, att, ys, yq, wo, ln_g, ln_b, wr_t, br, layer, alpha, tm):
    n, d = x.shape
    row = lambda w: pl.BlockSpec((tm, w), lambda i: (i, 0))
    return pl.pallas_call(
        functools.partial(_outproj_kernel, alpha),
        grid=(n // tm,),
        in_specs=[row(d), row(D_ATT), row(D_SSM), row(D_POOL), _layer_spec(wo, layer),
                  _layer_spec(ln_g, layer), _layer_spec(ln_b, layer), _shared_spec(wr_t), _shared_spec(br)],
        out_specs=[row(d), pl.BlockSpec((N_EXPERTS, tm), lambda i: (0, i))],
        out_shape=[jax.ShapeDtypeStruct((n, d), F32), jax.ShapeDtypeStruct((N_EXPERTS, n), F32)],
        compiler_params=_cparams(1),
        name="outproj_ln_router",
    )(x, att, ys, yq, wo, ln_g, ln_b, wr_t, br)


def _moe_kernel(alpha, d_expert, h_ref, gates_ref, wg_ref, wu_ref, wd_ref, g_ref, b_ref, o_ref,
                hb_ref, acc_ref):
    g = pl.program_id(1)

    @pl.when(g == 0)
    def _():
        hb_ref[...] = h_ref[...].astype(BF16)
        acc_ref[...] = jnp.zeros_like(acc_ref)

    hb = hb_ref[...]
    a = _dot(hb, wg_ref[...])
    b = _dot(hb, wu_ref[...])
    gates = gates_ref[0]
    parts = []
    for e in range(EXPERTS_PER_GROUP):
        cs = slice(e * d_expert, (e + 1) * d_expert)
        hid = jax.nn.silu(a[:, cs]) * b[:, cs] * gates[:, e:e + 1]
        parts.append(hid.astype(BF16))
    acc_ref[...] += _dot(jnp.concatenate(parts, axis=1), wd_ref[...])

    @pl.when(g == N_EXPERT_GROUPS - 1)
    def _():
        y = alpha * h_ref[...] + acc_ref[...]
        o_ref[...] = _layer_norm(y, g_ref[...], b_ref[...])


def _moe(h, gates_g, wg, wu, wd, ln_g, ln_b, layer, alpha, tm):
    n, d = h.shape
    gw = wg.shape[2] // N_EXPERT_GROUPS
    d_expert = gw // EXPERTS_PER_GROUP
    return pl.pallas_call(
        functools.partial(_moe_kernel, alpha, d_expert),
        grid=(n // tm, N_EXPERT_GROUPS),
        in_specs=[pl.BlockSpec((tm, d), lambda i, j: (i, 0)),
                  pl.BlockSpec((1, tm, EXPERTS_PER_GROUP), lambda i, j: (j, i, 0)),
                  pl.BlockSpec((None, d, gw), lambda i, j: (layer, 0, j)),
                  pl.BlockSpec((None, d, gw), lambda i, j: (layer, 0, j)),
                  pl.BlockSpec((None, gw, d), lambda i, j: (layer, j, 0)),
                  _layer_spec(ln_g, layer), _layer_spec(ln_b, layer)],
        out_specs=pl.BlockSpec((tm, d), lambda i, j: (i, 0)),
        out_shape=jax.ShapeDtypeStruct((n, d), F32),
        scratch_shapes=[pltpu.VMEM((tm, d), BF16), pltpu.VMEM((tm, d), F32)],
        compiler_params=_cparams(2),
        name="moe_ln",
    )(h, gates_g, wg, wu, wd, ln_g, ln_b)


def _post_block(x, att, ys, yq, lw, layer, alpha, tm):
    n = x.shape[0]
    h, gates_t = _outproj(x, att, ys, yq, lw["wo"], lw["ln1_g"], lw["ln1_b"], lw["wr_t"], lw["br"], layer,
                          alpha, tm)
    gates_g = gates_t.reshape(N_EXPERT_GROUPS, EXPERTS_PER_GROUP, n).transpose(0, 2, 1)
    return _moe(h, gates_g, lw["wg"], lw["wu"], lw["wd"], lw["ln2_g"], lw["ln2_b"], layer, alpha, tm)


def kernel(x_prompt, x_sample, cache_k, cache_v, cache_logf, page_table, state_ssm_re, state_ssm_im, state_pool, meta_tokens, ln0_g, ln0_b, w_in, b_forget, ssm_lam_re, ssm_lam_im, ssm_log_dt, ssm_b_re, ssm_b_im, ssm_c_re, ssm_c_im, ssm_d, w_glu, b_glu, w_pool, s_pool, w_out, ln1_g, ln1_b, w_router, b_router, w_gate, w_up, w_down, ln2_g, ln2_b):
    bp, seq, d_model = x_prompt.shape
    bs, t_new, _ = x_sample.shape
    depth = w_in.shape[0]
    n_pages, page = page_table.shape[1], cache_k.shape[2]
    past_len = n_pages * page
    length = N_META + seq
    lp = _round_up(length, LANES)
    tile = _pick_tile(lp, (640, 512, 384, 256, 128))
    alpha = (2 * depth) ** 0.25
    n_s = bs * t_new
    tm_s = _pick_tile(n_s, (512, 256, 128, 64, 32, 16, 8))
    assert t_new == SUBLANES and page == LANES and past_len >= POOL_BUF
    assert bs % DECODE_SEQS_PER_STEP == 0

    o1, o2, o3 = D_ATT, 2 * D_ATT, 3 * D_ATT
    o4 = o3 + N_HEADS
    row2 = lambda v: v.reshape(1, -1)

    kt_pages = cache_k.transpose(0, 1, 3, 4, 2).reshape(-1, depth, D_ATT, page)
    vt_pages = cache_v.transpose(0, 1, 3, 4, 2).reshape(-1, depth, D_ATT, page)
    lf_pages = cache_logf.transpose(0, 1, 3, 2)

    meta = jnp.broadcast_to(meta_tokens[None], (bp, N_META, d_model))
    xp = jnp.pad(x_prompt, ((0, 0), (N_META, lp - length), (0, 0)))
    xp = lax.dynamic_update_slice(xp, meta, (0, 0, 0))
    xs = x_sample.reshape(n_s, d_model)
    wr_hi = w_router.T.astype(BF16)
    wr_lo = (w_router.T - wr_hi.astype(F32)).astype(BF16)
    wr_t = jnp.concatenate([wr_hi, wr_lo], axis=0)
    br = b_router.reshape(N_EXPERTS, 1)
    nb_s = _pick_tile(bs, (16, 8, 4, 2, 1))

    outs_p = {k: [] for k in ("k", "v", "lf", "sr", "si", "pb")}
    outs_s = {k: [] for k in ("k", "v", "lf", "sr", "si", "pb")}
    rows3 = lambda v: v.reshape(depth, 1, -1)
    wb = w_in.astype(BF16)
    wq = wb[:, :, :o1]
    wup = wb[:, :, o4:]
    wt = jnp.concatenate([wb[:, :, o1:o3], wb[:, :, o3:o4]], axis=2).transpose(0, 2, 1)
    w_nat = jnp.concatenate([wb[:, :, :o3], wb[:, :, o4:]], axis=2)
    wf_t = wb[:, :, o3:o4].transpose(0, 2, 1)
    bf = b_forget.reshape(depth, N_HEADS, 1)

    apow_re, apow_im, bbr_t, bbi_t = _s5_params(ssm_lam_re, ssm_lam_im, ssm_log_dt, ssm_b_re, ssm_b_im)
    apr = apow_re.reshape(depth, SUBLANES, D_STATE)
    api = apow_im.reshape(depth, SUBLANES, D_STATE)
    ridx = jnp.arange(SUBLANES)[None, :, None]
    ak = []
    for k in (1, 2, 4):
        ak.append(jnp.where(ridx >= k, apr[:, k - 1][:, None], 0.0))
        ak.append(jnp.where(ridx >= k, api[:, k - 1][:, None], 0.0))
    ak += [apr, api]
    mix_w = dict(
        bbr=_block_diag(bbr_t).astype(BF16), bbi=_block_diag(bbi_t).astype(BF16),
        cr=_block_diag(ssm_c_re.transpose(0, 1, 3, 2)).astype(BF16),
        ci=_block_diag(ssm_c_im.transpose(0, 1, 3, 2)).astype(BF16),
        d=rows3(ssm_d), ak=jnp.stack(ak, axis=1), wglu=w_glu.astype(BF16), bglu=rows3(b_glu),
        wpool=_block_diag(w_pool).astype(BF16), spool=rows3(s_pool))
    d_expert = w_gate.shape[-1]
    lw = dict(
        wo=w_out.astype(BF16), ln1_g=rows3(ln1_g), ln1_b=rows3(ln1_b), wr_t=wr_t, br=br,
        wg=w_gate.transpose(0, 2, 1, 3).reshape(depth, d_model, N_EXPERTS * d_expert).astype(BF16),
        wu=w_up.transpose(0, 2, 1, 3).reshape(depth, d_model, N_EXPERTS * d_expert).astype(BF16),
        wd=w_down.reshape(depth, N_EXPERTS * d_expert, d_model).astype(BF16),
        ln2_g=rows3(ln2_g), ln2_b=rows3(ln2_b))

    for l in range(depth):
        res = _inproj_prompt(xp, row2(ln0_g), row2(ln0_b), wq, wup, wt, bf, l, l == 0, tile)
        if l == 0:
            xp, *res = res
        q, kt, vt, ka, vb, lft, u, p = res
        att = _attn_prompt(q, ka, vb, tile)
        ys, yq, sr, si = _seqmix(u, p, mix_w, l, length, tile)
        xp = _post_block(xp.reshape(bp * lp, d_model), att.reshape(bp * lp, D_ATT),
                         ys.reshape(bp * lp, D_SSM), yq.reshape(bp * lp, D_POOL), lw, l, alpha,
                         tile).reshape(bp, lp, d_model)
        outs_p["k"].append(kt[:, :, :length])
        outs_p["v"].append(vt[:, :, :length])
        outs_p["lf"].append(lft[:, :, :length])
        outs_p["sr"].append(sr.reshape(bp, N_SSM_GROUPS, SSM_STATE))
        outs_p["si"].append(si.reshape(bp, N_SSM_GROUPS, SSM_STATE))
        outs_p["pb"].append(p[:, length - POOL_BUF:length])

        res = _inproj_sample(xs, row2(ln0_g), row2(ln0_b), w_nat, wf_t, bf, l, l == 0, tm_s)
        if l == 0:
            xs, *res = res
        qkv, up, lfs_t = res
        lf_new = lfs_t.reshape(N_HEADS, bs, t_new).transpose(1, 0, 2)
        lf_new_pad = jnp.pad(lf_new, ((0, 0), (0, 0), (0, LANES - t_new)))
        att = _attn_decode(page_table, qkv.reshape(bs, t_new, 3 * D_ATT), lf_new_pad,
                           kt_pages, vt_pages, lf_pages, l)
        hist = jnp.concatenate([jnp.zeros((bs, 1, D_POOL), F32), state_pool[:, l]], axis=1)
        up3 = up.reshape(bs, t_new, D_SSM + D_POOL)
        p_s = up3[..., D_SSM:]
        ys, yq, sr, si = _seqmix_sample(up3, state_ssm_re[:, l].reshape(bs, 1, D_STATE),
                                        state_ssm_im[:, l].reshape(bs, 1, D_STATE), hist, mix_w, l, past_len,
                                        nb_s)
        xs = _post_block(xs, att.reshape(n_s, D_ATT), ys.reshape(n_s, D_SSM), yq.reshape(n_s, D_POOL),
                         lw, l, alpha, tm_s)
        qkv3 = qkv.reshape(bs, t_new, 3, N_HEADS, HEAD_DIM)
        outs_s["k"].append(qkv3[:, :, 1])
        outs_s["v"].append(qkv3[:, :, 2])
        outs_s["lf"].append(lf_new.transpose(0, 2, 1))
        outs_s["sr"].append(sr.reshape(bs, N_SSM_GROUPS, SSM_STATE))
        outs_s["si"].append(si.reshape(bs, N_SSM_GROUPS, SSM_STATE))
        outs_s["pb"].append(jnp.concatenate([state_pool[:, l], p_s], axis=1)[:, -POOL_BUF:])

    def heads_last(xs_t):
        a = jnp.stack(xs_t, axis=1)
        b_, dep, _, ln = a.shape
        return a.reshape(b_, dep, N_HEADS, HEAD_DIM, ln).transpose(0, 1, 4, 2, 3)

    y_prompt = xp[:, N_META:length]
    y_sample = xs.reshape(bs, t_new, d_model)
    return (y_prompt, y_sample,
            heads_last(outs_p["k"]), heads_last(outs_p["v"]),
            jnp.stack(outs_p["lf"], axis=1).transpose(0, 1, 3, 2),
            jnp.stack(outs_p["sr"], axis=1), jnp.stack(outs_p["si"], axis=1), jnp.stack(outs_p["pb"], axis=1),
            jnp.stack(outs_s["k"], axis=1), jnp.stack(outs_s["v"], axis=1), jnp.stack(outs_s["lf"], axis=1),
            jnp.stack(outs_s["sr"], axis=1), jnp.stack(outs_s["si"], axis=1), jnp.stack(outs_s["pb"], axis=1))
```

```python
import functools
import math

import jax
import jax.numpy as jnp
from jax import lax
from jax.experimental import pallas as pl
from jax.experimental.pallas import tpu as pltpu

F32 = jnp.float32
BF16 = jnp.bfloat16

N_META = 16
N_HEADS = 8
HEAD_DIM = 64
D_ATT = N_HEADS * HEAD_DIM
SSM_GROUP = 16
N_SSM_GROUPS = 16
SSM_STATE = 64
D_SSM = SSM_GROUP * N_SSM_GROUPS
D_STATE = N_SSM_GROUPS * SSM_STATE
POOL_WINDOWS = (2, 4, 8, 16)
POOL_GROUP = 64
D_POOL = POOL_GROUP * len(POOL_WINDOWS)
POOL_BUF = max(POOL_WINDOWS) - 1
POOL_HIST = POOL_BUF + 1
N_EXPERTS = 16
N_EXPERT_GROUPS = 4
EXPERTS_PER_GROUP = N_EXPERTS // N_EXPERT_GROUPS
TOP_K = 2
SCALE = HEAD_DIM ** -0.5
LOG2E = math.log2(math.e)
LN_EPS = 1e-5
NEG_INF = -1e30
HEADS_PER_STEP = 4
BIAS_PIECES = 3
SOFTMAX_ROWS = 32
DECODE_SEQS_PER_STEP = 2
MOE_EXPERTS_PER_STEP = 8
MOE_STEPS = N_EXPERTS // MOE_EXPERTS_PER_STEP

LANES = 128
SUBLANES = 8
VMEM_LIMIT = 56 * 1024 * 1024


def _cparams(n_axes):
    return pltpu.CompilerParams(
        dimension_semantics=("arbitrary",) * n_axes, vmem_limit_bytes=VMEM_LIMIT)


def _round_up(x, m):
    return (x + m - 1) // m * m


def _shared_spec(a):
    return pl.BlockSpec(a.shape, lambda *_: (0,) * a.ndim)


def _layer_spec(a, layer):
    return pl.BlockSpec((None,) + a.shape[1:], lambda *_: (layer,) + (0,) * (a.ndim - 1))


def _pick_tile(n, candidates):
    for c in candidates:
        if n % c == 0:
            return c
    raise ValueError(f"no tile for {n}")


def _layer_norm(x, g, b):
    mu = jnp.mean(x, axis=-1, keepdims=True)
    xc = x - mu
    var = jnp.mean(xc * xc, axis=-1, keepdims=True)
    return xc * lax.rsqrt(var + LN_EPS) * g + b


def _log_sigmoid(x):
    return -(jnp.maximum(-x, 0.0) + jnp.log1p(jnp.exp(-jnp.abs(x))))


def _dot(a, b):
    return jnp.dot(a, b, preferred_element_type=F32)


def _dot_nt(a, b, precision=None):
    return lax.dot_general(a, b, (((1,), (1,)), ((), ())),
                           preferred_element_type=F32, precision=precision)


def _lane_cumsum(x):
    lane = lax.broadcasted_iota(jnp.int32, x.shape, 1)
    s = 1
    while s < LANES:
        x = x + jnp.where(lane >= s, pltpu.roll(x, s, axis=1), 0.0)
        s *= 2
    return x


def _s5_param_kernel(lr_ref, li_ref, ldt_ref, br_ref, bi_ref, apow_re, apow_im, bbr_ref, bbi_ref):
    lr = lr_ref[...]
    li = li_ref[...]
    dt = jnp.exp(ldt_ref[...])
    mag = jnp.exp(lr * dt)
    ab_re = mag * jnp.cos(li * dt)
    ab_im = mag * jnp.sin(li * dt)
    nr, ni = ab_re - 1.0, ab_im
    den = lr * lr + li * li
    gr = (nr * lr + ni * li) / den
    gi = (ni * lr - nr * li) / den
    br = br_ref[...]
    bi = bi_ref[...]
    bbr_ref[...] = gr[:, None, :] * br - gi[:, None, :] * bi
    bbi_ref[...] = gr[:, None, :] * bi + gi[:, None, :] * br
    pr, pi = ab_re, ab_im
    apow_re[0] = pr
    apow_im[0] = pi
    for k in range(1, SUBLANES):
        pr, pi = pr * ab_re - pi * ab_im, pr * ab_im + pi * ab_re
        apow_re[k] = pr
        apow_im[k] = pi


def _s5_params(lam_re, lam_im, log_dt, b_re, b_im):
    depth, g, p = lam_re.shape
    c = b_re.shape[-1]
    per_layer = lambda *dims: pl.BlockSpec((None,) + dims, lambda l: (l,) + (0,) * len(dims))
    out_shape = (jax.ShapeDtypeStruct((depth, SUBLANES, g, p), F32),
                 jax.ShapeDtypeStruct((depth, SUBLANES, g, p), F32),
                 jax.ShapeDtypeStruct((depth, g, c, p), F32), jax.ShapeDtypeStruct((depth, g, c, p), F32))
    return pl.pallas_call(
        _s5_param_kernel,
        grid=(depth,),
        in_specs=[per_layer(g, p), per_layer(g, p), per_layer(g, 1), per_layer(g, c, p), per_layer(g, c, p)],
        out_specs=[per_layer(SUBLANES, g, p), per_layer(SUBLANES, g, p), per_layer(g, c, p),
                   per_layer(g, c, p)],
        out_shape=out_shape,
        compiler_params=_cparams(1),
        name="s5_params",
    )(lam_re, lam_im, log_dt.reshape(depth, g, 1), b_re.transpose(0, 1, 3, 2), b_im.transpose(0, 1, 3, 2))


def _block_diag(w):
    dep, g, a, b = w.shape
    eye = jnp.eye(g, dtype=w.dtype)
    return (w[:, :, :, None, :] * eye[None, :, None, :, None]).reshape(dep, g * a, g * b)


def _inproj_prompt_kernel(apply_ln, tm, x_ref, g_ref, b_ref, wq_ref, wup_ref, wt_ref, bf_ref, *outs):
    if apply_ln:
        xn_ref, q_ref, kt_ref, vt_ref, ka_ref, vb_ref, lft_ref, u_ref, p_ref, carry_ref, ck_scr = outs
    else:
        q_ref, kt_ref, vt_ref, ka_ref, vb_ref, lft_ref, u_ref, p_ref, carry_ref, ck_scr = outs
    t = pl.program_id(1)
    x = x_ref[0]
    if apply_ln:
        x = _layer_norm(x, g_ref[...], b_ref[...])
        xn_ref[0] = x
    xb = x.astype(BF16)
    q = _dot(xb, wq_ref[...]) * (SCALE * LOG2E)
    lane = lax.broadcasted_iota(jnp.int32, (tm, LANES), 1)
    ones = jnp.where(lane < HEAD_DIM + BIAS_PIECES, 1.0, 0.0)
    for h in range(N_HEADS):
        pair = q[:, (h // 2) * LANES:(h // 2 + 1) * LANES]
        if h % 2:
            pair = pltpu.roll(pair, HEAD_DIM, axis=1)
        q_ref[0, h] = jnp.where(lane < HEAD_DIM, pair, ones).astype(BF16)
    up = _dot(xb, wup_ref[...])
    u_ref[0] = up[:, :D_SSM]
    p_ref[0] = up[:, D_SSM:]
    zt = _dot_nt(wt_ref[...], xb)
    kt = zt[:D_ATT]
    vt = zt[D_ATT:2 * D_ATT]
    kt_ref[0] = kt
    vt_ref[0] = vt
    ones_row = lax.broadcasted_iota(jnp.int32, (HEAD_DIM, tm), 0) == 0
    for h in range(N_HEADS):
        vb_ref[0, h, 0:HEAD_DIM, :] = vt[h * HEAD_DIM:(h + 1) * HEAD_DIM].astype(BF16)
        vb_ref[0, h, HEAD_DIM:, :] = jnp.where(ones_row, 1.0, 0.0).astype(BF16)
    lf = _log_sigmoid(zt[2 * D_ATT:] + bf_ref[...])
    lft_ref[0] = lf

    @pl.when(t == 0)
    def _():
        carry_ref[...] = jnp.zeros_like(carry_ref)

    carry = carry_ref[...]
    for c in range(tm // LANES):
        blk = _lane_cumsum(lf[:, c * LANES:(c + 1) * LANES]) + carry
        ck_scr[:, c * LANES:(c + 1) * LANES] = blk
        carry = jnp.broadcast_to(blk[:, LANES - 1:LANES], carry.shape)
    carry_ref[...] = carry

    piece_row = lax.broadcasted_iota(jnp.int32, (HEAD_DIM, tm), 0)
    for h in range(N_HEADS):
        ka_ref[0, h, 0:HEAD_DIM, :] = kt[h * HEAD_DIM:(h + 1) * HEAD_DIM].astype(BF16)
        rest = ck_scr[h:h + 1, :] * (-LOG2E)
        aug = jnp.zeros((HEAD_DIM, tm), F32)
        for i in range(BIAS_PIECES):
            piece = rest.astype(BF16).astype(F32)
            aug = jnp.where(piece_row == i, piece, aug)
            rest = rest - piece
        ka_ref[0, h, HEAD_DIM:, :] = aug.astype(BF16)


def _inproj_prompt(x, ln_g, ln_b, wq, wup, wt, bf, layer, apply_ln, tm):
    b, lp, d = x.shape
    nt = lp // tm
    out_shape, out_specs = [], []
    if apply_ln:
        out_shape.append(jax.ShapeDtypeStruct((b, lp, d), F32))
        out_specs.append(pl.BlockSpec((1, tm, d), lambda i, j: (i, j, 0)))
    out_shape += [
        jax.ShapeDtypeStruct((b, N_HEADS, lp, LANES), BF16),
        jax.ShapeDtypeStruct((b, D_ATT, lp), F32),
        jax.ShapeDtypeStruct((b, D_ATT, lp), F32),
        jax.ShapeDtypeStruct((b, N_HEADS, LANES, lp), BF16),
        jax.ShapeDtypeStruct((b, N_HEADS, LANES, lp), BF16),
        jax.ShapeDtypeStruct((b, N_HEADS, lp), F32),
        jax.ShapeDtypeStruct((b, lp, D_SSM), F32),
        jax.ShapeDtypeStruct((b, lp, D_POOL), F32),
    ]
    out_specs += [
        pl.BlockSpec((1, N_HEADS, tm, LANES), lambda i, j: (i, 0, j, 0)),
        pl.BlockSpec((1, D_ATT, tm), lambda i, j: (i, 0, j)),
        pl.BlockSpec((1, D_ATT, tm), lambda i, j: (i, 0, j)),
        pl.BlockSpec((1, N_HEADS, LANES, tm), lambda i, j: (i, 0, 0, j)),
        pl.BlockSpec((1, N_HEADS, LANES, tm), lambda i, j: (i, 0, 0, j)),
        pl.BlockSpec((1, N_HEADS, tm), lambda i, j: (i, 0, j)),
        pl.BlockSpec((1, tm, D_SSM), lambda i, j: (i, j, 0)),
        pl.BlockSpec((1, tm, D_POOL), lambda i, j: (i, j, 0)),
    ]
    return pl.pallas_call(
        functools.partial(_inproj_prompt_kernel, apply_ln, tm),
        grid=(b, nt),
        in_specs=[pl.BlockSpec((1, tm, d), lambda i, j: (i, j, 0)),
                  _shared_spec(ln_g), _shared_spec(ln_b), _layer_spec(wq, layer), _layer_spec(wup, layer),
                  _layer_spec(wt, layer), _layer_spec(bf, layer)],
        out_specs=out_specs,
        out_shape=out_shape,
        scratch_shapes=[pltpu.VMEM((N_HEADS, LANES), F32), pltpu.VMEM((N_HEADS, tm), F32)],
        compiler_params=_cparams(2),
        name="inproj_prompt",
    )(x, ln_g, ln_b, wq, wup, wt, bf)


def _inproj_sample_kernel(apply_ln, x_ref, g_ref, b_ref, w_ref, wf_ref, bf_ref, *outs):
    if apply_ln:
        xn_ref, qkv_ref, up_ref, lft_ref = outs
    else:
        qkv_ref, up_ref, lft_ref = outs
    x = x_ref[...]
    if apply_ln:
        x = _layer_norm(x, g_ref[...], b_ref[...])
        xn_ref[...] = x
    xb = x.astype(BF16)
    z = _dot(xb, w_ref[...])
    qkv_ref[...] = z[:, :3 * D_ATT]
    up_ref[...] = z[:, 3 * D_ATT:]
    lft_ref[...] = _log_sigmoid(_dot_nt(wf_ref[...], xb) + bf_ref[...])


def _inproj_sample(x, ln_g, ln_b, w, wf, bf, layer, apply_ln, tm):
    n, d = x.shape
    out_shape, out_specs = [], []
    if apply_ln:
        out_shape.append(jax.ShapeDtypeStruct((n, d), F32))
        out_specs.append(pl.BlockSpec((tm, d), lambda i: (i, 0)))
    out_shape += [jax.ShapeDtypeStruct((n, 3 * D_ATT), F32),
                  jax.ShapeDtypeStruct((n, D_SSM + D_POOL), F32),
                  jax.ShapeDtypeStruct((N_HEADS, n), F32)]
    out_specs += [pl.BlockSpec((tm, 3 * D_ATT), lambda i: (i, 0)),
                  pl.BlockSpec((tm, D_SSM + D_POOL), lambda i: (i, 0)),
                  pl.BlockSpec((N_HEADS, tm), lambda i: (0, i))]
    return pl.pallas_call(
        functools.partial(_inproj_sample_kernel, apply_ln),
        grid=(n // tm,),
        in_specs=[pl.BlockSpec((tm, d), lambda i: (i, 0)),
                  _shared_spec(ln_g), _shared_spec(ln_b), _layer_spec(w, layer), _layer_spec(wf, layer),
                  _layer_spec(bf, layer)],
        out_specs=out_specs,
        out_shape=out_shape,
        compiler_params=_cparams(1),
        name="inproj_sample",
    )(x, ln_g, ln_b, w, wf, bf)


def _attn_prompt_kernel(tq, q_ref, ka_ref, vb_ref, o_ref,
                        s_a, s_b, p_a, p_b, al_a, al_b, m_scr, acc_scr):
    qi = pl.program_id(2)
    rc = SOFTMAX_ROWS
    m_scr[...] = jnp.full(m_scr.shape, NEG_INF, F32)
    acc_scr[...] = jnp.zeros(acc_scr.shape, F32)

    def scores(k, s_buf, j):
        k0 = pl.multiple_of(k * tq, LANES)
        s_buf[j] = _dot(q_ref[0, j], ka_ref[0, j, :, pl.ds(k0, tq)])

    def chunk(s_buf, j, c, masked):
        rows = slice(c * rc, (c + 1) * rc)
        sc = s_buf[j, rows, :]
        if masked:
            row = lax.broadcasted_iota(jnp.int32, (rc, tq), 0)
            col = lax.broadcasted_iota(jnp.int32, (rc, tq), 1)
            sc = jnp.where(col <= row + c * rc, sc, NEG_INF)
        return rows, sc

    def row_max(s_buf, al_buf, j, masked):
        for c in range(tq // rc):
            rows, sc = chunk(s_buf, j, c, masked)
            m_old = m_scr[j, rows, :]
            m_new = jnp.maximum(m_old, jnp.max(sc, axis=1, keepdims=True))
            al_buf[j, rows, :] = jnp.exp2(m_old - m_new)
            m_scr[j, rows, :] = m_new

    def probs(s_buf, p_buf, j, masked):
        for c in range(tq // rc):
            rows, sc = chunk(s_buf, j, c, masked)
            x = sc - jnp.concatenate([m_scr[j, rows, :]] * (tq // LANES), axis=1)
            p_buf[j, rows, :] = jnp.exp2(x.astype(BF16))

    def values(k, p_buf, al_buf, j):
        k0 = pl.multiple_of(k * tq, LANES)
        acc_scr[j] = al_buf[j] * acc_scr[j] + _dot_nt(p_buf[j], vb_ref[0, j, :, pl.ds(k0, tq)])

    for j in range(HEADS_PER_STEP):
        scores(0, s_a, j)
    p_b[...] = jnp.zeros(p_b.shape, BF16)
    al_b[...] = jnp.ones(al_b.shape, F32)

    def stage(k, s_cur, s_nxt, p_cur, p_prev, al_cur, al_prev):
        k_prev = jnp.maximum(k - 1, 0)
        for j in range(HEADS_PER_STEP):
            scores(k + 1, s_nxt, j)
        for j in range(HEADS_PER_STEP):
            row_max(s_cur, al_cur, j, False)
        for j in range(HEADS_PER_STEP):
            probs(s_cur, p_cur, j, False)
        for j in range(HEADS_PER_STEP):
            values(k_prev, p_prev, al_prev, j)

    def body(k, carry):
        @pl.when(k % 2 == 0)
        def _():
            stage(k, s_a, s_b, p_a, p_b, al_a, al_b)

        @pl.when(k % 2 == 1)
        def _():
            stage(k, s_b, s_a, p_b, p_a, al_b, al_a)

        return carry

    lax.fori_loop(0, qi, body, 0)

    def drain(s_cur, p_cur, p_prev, al_cur, al_prev):
        k_prev = jnp.maximum(qi - 1, 0)
        for j in range(HEADS_PER_STEP):
            row_max(s_cur, al_cur, j, True)
        for j in range(HEADS_PER_STEP):
            values(k_prev, p_prev, al_prev, j)
            probs(s_cur, p_cur, j, True)
        for j in range(HEADS_PER_STEP):
            values(qi, p_cur, al_cur, j)

    @pl.when(qi % 2 == 0)
    def _():
        drain(s_a, p_a, p_b, al_a, al_b)

    @pl.when(qi % 2 == 1)
    def _():
        drain(s_b, p_b, p_a, al_b, al_a)

    outs = []
    for j in range(HEADS_PER_STEP):
        acc = acc_scr[j]
        outs.append(acc[:, 0:HEAD_DIM] / acc[:, HEAD_DIM:HEAD_DIM + 1])
    o_ref[0] = jnp.concatenate(outs, axis=1).astype(o_ref.dtype)


def _attn_prompt(q, ka, vb, tq):
    b, _, lp, _ = q.shape
    hp = N_HEADS // HEADS_PER_STEP
    rows = HEADS_PER_STEP * HEAD_DIM
    return pl.pallas_call(
        functools.partial(_attn_prompt_kernel, tq),
        grid=(b, hp, lp // tq),
        in_specs=[pl.BlockSpec((1, HEADS_PER_STEP, tq, LANES), lambda i, h, j: (i, h, j, 0)),
                  pl.BlockSpec((1, HEADS_PER_STEP, LANES, lp), lambda i, h, j: (i, h, 0, 0),
                               pipeline_mode=pl.Buffered(1)),
                  pl.BlockSpec((1, HEADS_PER_STEP, LANES, lp), lambda i, h, j: (i, h, 0, 0),
                               pipeline_mode=pl.Buffered(1))],
        out_specs=pl.BlockSpec((1, tq, rows), lambda i, h, j: (i, j, h)),
        out_shape=jax.ShapeDtypeStruct((b, lp, D_ATT), BF16),
        scratch_shapes=[pltpu.VMEM((HEADS_PER_STEP, tq, tq), F32), pltpu.VMEM((HEADS_PER_STEP, tq, tq), F32),
                        pltpu.VMEM((HEADS_PER_STEP, tq, tq), BF16), pltpu.VMEM((HEADS_PER_STEP, tq, tq), BF16),
                        pltpu.VMEM((HEADS_PER_STEP, tq, LANES), F32), pltpu.VMEM((HEADS_PER_STEP, tq, LANES), F32),
                        pltpu.VMEM((HEADS_PER_STEP, tq, LANES), F32),
                        pltpu.VMEM((HEADS_PER_STEP, tq, LANES), F32)],
        compiler_params=_cparams(3),
        name="attn_prompt",
    )(q, ka, vb)


def _attn_decode_kernel(n_pages, t_new, layer, n_steps, pt_ref, qkv_ref, lfn_ref, lf_ref, kt_hbm, vt_hbm,
                        o_ref, kbuf, vbuf, ksem, vsem):
    step = pl.program_id(0)
    slot = lax.rem(step, 2)
    n_cp = DECODE_SEQS_PER_STEP * n_pages

    def page_copies(s, slot_):
        cps = []
        for i in range(n_cp):
            page_id = pt_ref[s * n_cp + i]
            cps.append(pltpu.make_async_copy(kt_hbm.at[page_id, layer], kbuf.at[slot_, i], ksem.at[slot_]))
            cps.append(pltpu.make_async_copy(vt_hbm.at[page_id, layer], vbuf.at[slot_, i], vsem.at[slot_]))
        return cps

    @pl.when(step == 0)
    def _():
        for cp in page_copies(0, 0):
            cp.start()

    @pl.when(step + 1 < n_steps)
    def _():
        for cp in page_copies(step + 1, 1 - slot):
            cp.start()

    for cp in page_copies(step, slot):
        cp.wait()

    rows = t_new * N_HEADS
    head_of_row = lax.broadcasted_iota(jnp.int32, (N_HEADS, D_ATT), 0)
    head_of_col = lax.broadcasted_iota(jnp.int32, (N_HEADS, D_ATT), 1) // HEAD_DIM
    hmask = head_of_row == head_of_col
    omask = jnp.concatenate([hmask] * t_new, axis=0)
    key_t = lax.broadcasted_iota(jnp.int32, (rows, LANES), 1)
    qry_t = lax.broadcasted_iota(jnp.int32, (rows, LANES), 0) // N_HEADS
    pad = jnp.zeros((LANES - t_new, D_ATT), F32)

    def expand(c):
        return jnp.concatenate([c] * t_new, axis=0)

    for sq in range(DECODE_SEQS_PER_STEP):
        base = n_pages * sq
        pt0 = step * n_cp + base
        qkv = qkv_ref[sq]
        q = qkv[:, :D_ATT] * SCALE
        k_new = qkv[:, D_ATT:2 * D_ATT]
        v_new = qkv[:, 2 * D_ATT:]
        qbd = jnp.concatenate(
            [jnp.where(hmask, jnp.broadcast_to(q[t:t + 1], (N_HEADS, D_ATT)), 0.0) for t in range(t_new)],
            axis=0).astype(BF16)

        local = _lane_cumsum(jnp.concatenate([lf_ref[pt_ref[pt0 + j], 0] for j in range(n_pages)]
                                             + [lfn_ref[sq]], axis=0))
        off = jnp.zeros((N_HEADS, LANES), F32)
        s_blocks = []
        for j in range(n_pages):
            ck = local[j * N_HEADS:(j + 1) * N_HEADS] + off
            off = jnp.broadcast_to(ck[:, LANES - 1:LANES], off.shape)
            s = _dot(qbd, kbuf[slot, base + j].astype(BF16))
            s_blocks.append(s - expand(ck))
        k_pad = jnp.concatenate([k_new, pad], axis=0).astype(BF16)
        v_pad = jnp.concatenate([v_new, pad], axis=0).astype(BF16)
        ck_new = local[n_pages * N_HEADS:] + off
        s_new = _dot_nt(qbd, k_pad) - expand(ck_new)
        s_blocks.append(jnp.where(key_t <= qry_t, s_new, NEG_INF))

        m = s_blocks[0]
        for s in s_blocks[1:]:
            m = jnp.maximum(m, s)
        m = m.max(axis=1, keepdims=True)
        l = jnp.zeros((rows, LANES), F32)
        acc = jnp.zeros((rows, D_ATT), F32)
        for j, s in enumerate(s_blocks):
            p = jnp.exp(s - m)
            l = l + p
            if j < n_pages:
                acc = acc + _dot_nt(p.astype(BF16), vbuf[slot, base + j].astype(BF16))
            else:
                acc = acc + _dot(p.astype(BF16), v_pad)
        acc = acc / l.sum(axis=1, keepdims=True)
        acc = jnp.where(omask, acc, 0.0)
        o_ref[sq] = acc.reshape(t_new, N_HEADS, D_ATT).sum(axis=1).astype(o_ref.dtype)


def _attn_decode(page_table, qkv, lf_new_t, kt_pages, vt_pages, lf_pages, layer):
    bs, t_new, _ = qkv.shape
    n_pages = page_table.shape[1]
    page = kt_pages.shape[-1]
    sps = DECODE_SEQS_PER_STEP
    n_steps = bs // sps
    pt_flat = page_table.reshape(-1)
    in_specs = [pl.BlockSpec((sps, t_new, 3 * D_ATT), lambda i, pt: (i, 0, 0)),
                pl.BlockSpec((sps, N_HEADS, LANES), lambda i, pt: (i, 0, 0)),
                pl.BlockSpec((lf_pages.shape[0], 1, N_HEADS, page), lambda i, pt: (0, layer, 0, 0),
                             pipeline_mode=pl.Buffered(1)),
                pl.BlockSpec(memory_space=pl.ANY),
                pl.BlockSpec(memory_space=pl.ANY)]
    page_buf = pltpu.VMEM((2, sps * n_pages, D_ATT, page), F32)
    grid_spec = pltpu.PrefetchScalarGridSpec(
        num_scalar_prefetch=1, grid=(n_steps,), in_specs=in_specs,
        out_specs=pl.BlockSpec((sps, t_new, D_ATT), lambda i, pt: (i, 0, 0)),
        scratch_shapes=[page_buf, page_buf, pltpu.SemaphoreType.DMA((2,)), pltpu.SemaphoreType.DMA((2,))])
    return pl.pallas_call(
        functools.partial(_attn_decode_kernel, n_pages, t_new, layer, n_steps),
        grid_spec=grid_spec,
        out_shape=jax.ShapeDtypeStruct((bs, t_new, D_ATT), BF16),
        compiler_params=_cparams(1),
        name="attn_decode",
    )(pt_flat, qkv, lf_new_t, lf_pages, kt_pages, vt_pages)


def _s5_scan_rows(br, bi, cr, ci, ak_ref):
    for idx, k in enumerate((1, 2, 4)):
        akr = ak_ref[2 * idx]
        aki = ak_ref[2 * idx + 1]
        sr = pltpu.roll(br, k, axis=0)
        si = pltpu.roll(bi, k, axis=0)
        br, bi = br + akr * sr - aki * si, bi + akr * si + aki * sr
    pr = ak_ref[6]
    pi = ak_ref[7]
    return br + pr * cr - pi * ci, bi + pr * ci + pi * cr


def _s5_readout(xr, xi, u, cr_ref, ci_ref, d_ref, wglu_ref, bglu_ref):
    y = _dot(xr.astype(BF16), cr_ref[...]) - _dot(xi.astype(BF16), ci_ref[...]) + d_ref[...] * u
    y = jax.nn.gelu(y)
    return y * jax.nn.sigmoid(_dot(y.astype(BF16), wglu_ref[...]) + bglu_ref[...])


def _pool_window_sums(e):
    c2 = e + pltpu.roll(e, 1, axis=0)
    c4 = c2 + pltpu.roll(c2, 2, axis=0)
    c8 = c4 + pltpu.roll(c4, 4, axis=0)
    c16 = c8 + pltpu.roll(c8, 8, axis=0)
    return c2, c4, c8, c16


def _pool_mix(sums, tok, pos, wpool_ref, spool_ref):
    c2, c4, c8, c16 = sums
    grp = lax.broadcasted_iota(jnp.int32, tok.shape, 1) // POOL_GROUP
    win = jnp.where(grp == 0, c2, jnp.where(grp == 1, c4, jnp.where(grp == 2, c8, c16)))
    wsize = jnp.where(grp == 0, 2, jnp.where(grp == 1, 4, jnp.where(grp == 2, 8, 16)))
    cnt = jnp.minimum(wsize, pos + 1).astype(F32)
    mm = win / cnt - tok
    return _dot(mm.astype(BF16), wpool_ref[...]) * spool_ref[...]


def _seqmix_kernel(tt, snap_tile, snap_row,
                   u_ref, p_ref, bbr_ref, bbi_ref, cr_ref, ci_ref, d_ref, ak_ref, wglu_ref, bglu_ref,
                   wpool_ref, spool_ref,
                   ys_ref, yq_ref, sr_ref, si_ref,
                   xr_scr, xi_scr, car_r, car_i, ext_scr):
    t = pl.program_id(1)

    @pl.when(t == 0)
    def _():
        car_r[...] = jnp.zeros_like(car_r)
        car_i[...] = jnp.zeros_like(car_i)
        ext_scr[0:POOL_HIST, :] = jnp.zeros((POOL_HIST, D_POOL), F32)

    u = u_ref[0]
    ub = u.astype(BF16)
    xr_scr[...] = _dot(ub, bbr_ref[...])
    xi_scr[...] = _dot(ub, bbi_ref[...])

    def scan_block(i, carry):
        cr, ci = carry
        r0 = pl.multiple_of(i * SUBLANES, SUBLANES)
        xr, xi = _s5_scan_rows(xr_scr[pl.ds(r0, SUBLANES), :], xi_scr[pl.ds(r0, SUBLANES), :], cr, ci, ak_ref)
        xr_scr[pl.ds(r0, SUBLANES), :] = xr
        xi_scr[pl.ds(r0, SUBLANES), :] = xi
        return xr[SUBLANES - 1:SUBLANES], xi[SUBLANES - 1:SUBLANES]

    cr, ci = lax.fori_loop(0, tt // SUBLANES, scan_block, (car_r[...], car_i[...]))
    car_r[...] = cr
    car_i[...] = ci

    @pl.when(t == snap_tile)
    def _():
        sr_ref[0] = xr_scr[snap_row:snap_row + 1, :]
        si_ref[0] = xi_scr[snap_row:snap_row + 1, :]

    ys_ref[0] = _s5_readout(xr_scr[...], xi_scr[...], u, cr_ref, ci_ref, d_ref, wglu_ref,
                            bglu_ref).astype(ys_ref.dtype)

    ext_scr[POOL_HIST:POOL_HIST + tt, :] = p_ref[0]
    e = ext_scr[...]
    sums = [c[POOL_HIST:] for c in _pool_window_sums(e)]
    pos = t * tt + lax.broadcasted_iota(jnp.int32, (tt, D_POOL), 0)
    yq_ref[0] = _pool_mix(sums, e[POOL_HIST:], pos, wpool_ref, spool_ref).astype(yq_ref.dtype)
    ext_scr[0:POOL_HIST, :] = e[tt:tt + POOL_HIST]


def _mix_weights(prm):
    return (prm["bbr"], prm["bbi"], prm["cr"], prm["ci"], prm["d"], prm["ak"], prm["wglu"],
            prm["bglu"], prm["wpool"], prm["spool"])


def _seqmix(u, p, prm, layer, length, tt):
    b, lp, _ = u.shape
    nt = lp // tt
    snap_tile, snap_row = (length - 1) // tt, (length - 1) % tt
    full = lambda a: _layer_spec(a, layer)
    weights = _mix_weights(prm)
    return pl.pallas_call(
        functools.partial(_seqmix_kernel, tt, snap_tile, snap_row),
        grid=(b, nt),
        in_specs=[pl.BlockSpec((1, tt, D_SSM), lambda i, j: (i, j, 0)),
                  pl.BlockSpec((1, tt, D_POOL), lambda i, j: (i, j, 0))]
                 + [full(w) for w in weights],
        out_specs=[pl.BlockSpec((1, tt, D_SSM), lambda i, j: (i, j, 0)),
                   pl.BlockSpec((1, tt, D_POOL), lambda i, j: (i, j, 0)),
                   pl.BlockSpec((1, 1, D_STATE), lambda i, j: (i, 0, 0)),
                   pl.BlockSpec((1, 1, D_STATE), lambda i, j: (i, 0, 0))],
        out_shape=[jax.ShapeDtypeStruct((b, lp, D_SSM), BF16),
                   jax.ShapeDtypeStruct((b, lp, D_POOL), BF16),
                   jax.ShapeDtypeStruct((b, 1, D_STATE), F32),
                   jax.ShapeDtypeStruct((b, 1, D_STATE), F32)],
        scratch_shapes=[pltpu.VMEM((tt, D_STATE), F32), pltpu.VMEM((tt, D_STATE), F32),
                        pltpu.VMEM((1, D_STATE), F32), pltpu.VMEM((1, D_STATE), F32),
                        pltpu.VMEM((POOL_HIST + tt, D_POOL), F32)],
        compiler_params=_cparams(2),
        name="seqmix",
    )(u, p, *weights)


def _seqmix_sample_kernel(nb, pos0, up_ref, x0r_ref, x0i_ref, hist_ref,
                          bbr_ref, bbi_ref, cr_ref, ci_ref, d_ref, ak_ref, wglu_ref, bglu_ref,
                          wpool_ref, spool_ref,
                          ys_ref, yq_ref, sr_ref, si_ref, xr_scr, xi_scr):
    t_new = SUBLANES
    up = up_ref[...].reshape(nb * t_new, D_SSM + D_POOL)
    u = up[:, :D_SSM]
    ub = u.astype(BF16)
    xr_scr[...] = _dot(ub, bbr_ref[...])
    xi_scr[...] = _dot(ub, bbi_ref[...])

    def scan_block(i, carry):
        r0 = pl.multiple_of(i * SUBLANES, SUBLANES)
        xr, xi = _s5_scan_rows(xr_scr[pl.ds(r0, SUBLANES), :], xi_scr[pl.ds(r0, SUBLANES), :],
                               x0r_ref[i], x0i_ref[i], ak_ref)
        xr_scr[pl.ds(r0, SUBLANES), :] = xr
        xi_scr[pl.ds(r0, SUBLANES), :] = xi
        sr_ref[i] = xr[SUBLANES - 1:SUBLANES]
        si_ref[i] = xi[SUBLANES - 1:SUBLANES]
        return carry

    lax.fori_loop(0, nb, scan_block, 0)
    ys = _s5_readout(xr_scr[...], xi_scr[...], u, cr_ref, ci_ref, d_ref, wglu_ref, bglu_ref)
    ys_ref[...] = ys.reshape(nb, t_new, D_SSM).astype(ys_ref.dtype)

    tok = up[:, D_SSM:].reshape(nb, t_new, D_POOL)
    per_seq = POOL_HIST + t_new
    e = jnp.concatenate([hist_ref[...], tok], axis=1).reshape(nb * per_seq, D_POOL)
    sums = [c.reshape(nb, per_seq, D_POOL)[:, POOL_HIST:].reshape(nb * t_new, D_POOL)
            for c in _pool_window_sums(e)]
    pos = pos0 + lax.broadcasted_iota(jnp.int32, (nb, t_new, D_POOL), 1).reshape(nb * t_new, D_POOL)
    yq = _pool_mix(sums, up[:, D_SSM:], pos, wpool_ref, spool_ref)
    yq_ref[...] = yq.reshape(nb, t_new, D_POOL).astype(yq_ref.dtype)


def _seqmix_sample(up, x0r, x0i, hist, prm, layer, pos0, nb):
    bs, t_new, width = up.shape
    full = lambda a: _layer_spec(a, layer)
    seq = lambda r, w: pl.BlockSpec((nb, r, w), lambda i: (i, 0, 0))
    weights = _mix_weights(prm)
    return pl.pallas_call(
        functools.partial(_seqmix_sample_kernel, nb, pos0),
        grid=(bs // nb,),
        in_specs=[seq(t_new, width), seq(1, D_STATE), seq(1, D_STATE), seq(POOL_HIST, D_POOL)]
                 + [full(w) for w in weights],
        out_specs=[seq(t_new, D_SSM), seq(t_new, D_POOL), seq(1, D_STATE), seq(1, D_STATE)],
        out_shape=[jax.ShapeDtypeStruct((bs, t_new, D_SSM), BF16),
                   jax.ShapeDtypeStruct((bs, t_new, D_POOL), BF16),
                   jax.ShapeDtypeStruct((bs, 1, D_STATE), F32),
                   jax.ShapeDtypeStruct((bs, 1, D_STATE), F32)],
        scratch_shapes=[pltpu.VMEM((nb * t_new, D_STATE), F32), pltpu.VMEM((nb * t_new, D_STATE), F32)],
        compiler_params=_cparams(1),
        name="seqmix_sample",
    )(up, x0r, x0i, hist, *weights)


def _outproj_kernel(alpha, x_ref, att_ref, ys_ref, yq_ref, wo_ref, g_ref, b_ref, wr_ref, br_ref,
                    h_ref, gates_ref):
    y = alpha * x_ref[...]
    y = y + _dot(att_ref[...], wo_ref[0:D_ATT, :])
    y = y + _dot(ys_ref[...], wo_ref[D_ATT:D_ATT + D_SSM, :])
    y = y + _dot(yq_ref[...], wo_ref[D_ATT + D_SSM:, :])
    h = _layer_norm(y, g_ref[...], b_ref[...])
    h_ref[...] = h
    h_hi = h.astype(BF16)
    h_lo = (h - h_hi.astype(F32)).astype(BF16)
    both = _dot_nt(wr_ref[...], h_hi)
    logits = both[:N_EXPERTS] + both[N_EXPERTS:] + _dot_nt(wr_ref[0:N_EXPERTS, :], h_lo)
    aff = jax.nn.sigmoid(logits)
    biased = aff + br_ref[...]
    rows = [biased[e:e + 1, :] for e in range(N_EXPERTS)]

    def beats(a, ia, c, ic):
        return (a >= c) if ia < ic else (a > c)

    gscore = []
    for g in range(N_EXPERT_GROUPS):
        members = list(range(g * EXPERTS_PER_GROUP, (g + 1) * EXPERTS_PER_GROUP))
        total = jnp.zeros_like(rows[0])
        for e in members:
            rank = jnp.zeros_like(rows[0])
            for o in members:
                if o != e:
                    rank = rank + beats(rows[o], o, rows[e], e).astype(F32)
            total = total + jnp.where(rank < 2.0, rows[e], 0.0)
        gscore.append(total)
    in_group = []
    for g in range(N_EXPERT_GROUPS):
        lost = jnp.zeros_like(rows[0])
        for o in range(N_EXPERT_GROUPS):
            if o != g:
                lost = lost + beats(gscore[o], o, gscore[g], g).astype(F32)
        in_group.append(lost < 1.0)
    masked = [jnp.where(in_group[e // EXPERTS_PER_GROUP], rows[e], NEG_INF) for e in range(N_EXPERTS)]
    sel = []
    for e in range(N_EXPERTS):
        rank = jnp.zeros_like(rows[0])
        for o in range(N_EXPERTS):
            if o != e:
                rank = rank + beats(masked[o], o, masked[e], e).astype(F32)
        sel.append(jnp.where(rank < float(TOP_K), aff[e:e + 1, :], 0.0))
    denom = sel[0]
    for e in range(1, N_EXPERTS):
        denom = denom + sel[e]
    gates_ref[...] = jnp.concatenate(sel, axis=0) / denom


def _outproj(x, att, ys, yq, wo, ln_g, ln_b, wr_t, br, layer, alpha, tm):
    n, d = x.shape
    row = lambda w: pl.BlockSpec((tm, w), lambda i: (i, 0))
    return pl.pallas_call(
        functools.partial(_outproj_kernel, alpha),
        grid=(n // tm,),
        in_specs=[row(d), row(D_ATT), row(D_SSM), row(D_POOL), _layer_spec(wo, layer),
                  _layer_spec(ln_g, layer), _layer_spec(ln_b, layer), _shared_spec(wr_t), _shared_spec(br)],
        out_specs=[row(d), pl.BlockSpec((N_EXPERTS, tm), lambda i: (0, i))],
        out_shape=[jax.ShapeDtypeStruct((n, d), F32), jax.ShapeDtypeStruct((N_EXPERTS, n), F32)],
        compiler_params=_cparams(1),
        name="outproj_ln_router",
    )(x, att, ys, yq, wo, ln_g, ln_b, wr_t, br)


def _moe_kernel(alpha, d_expert, h_ref, gates_ref, wg_ref, wu_ref, wd_ref, g_ref, b_ref, o_ref,
                hb_ref, acc_ref):
    g = pl.program_id(1)

    @pl.when(g == 0)
    def _():
        hb_ref[...] = h_ref[...].astype(BF16)
        acc_ref[...] = jnp.zeros_like(acc_ref)

    hb = hb_ref[...]
    a = _dot(hb, wg_ref[...])
    b = _dot(hb, wu_ref[...])
    gates = gates_ref[0]
    parts = []
    for e in range(MOE_EXPERTS_PER_STEP):
        cs = slice(e * d_expert, (e + 1) * d_expert)
        hid = jax.nn.silu(a[:, cs]) * b[:, cs] * gates[:, e:e + 1]
        parts.append(hid.astype(BF16))
    acc_ref[...] += _dot(jnp.concatenate(parts, axis=1), wd_ref[...])

    @pl.when(g == MOE_STEPS - 1)
    def _():
        y = alpha * h_ref[...] + acc_ref[...]
        o_ref[...] = _layer_norm(y, g_ref[...], b_ref[...])


def _moe(h, gates_g, wg, wu, wd, ln_g, ln_b, layer, alpha, tm):
    n, d = h.shape
    gw = wg.shape[2] // MOE_STEPS
    d_expert = gw // MOE_EXPERTS_PER_STEP
    return pl.pallas_call(
        functools.partial(_moe_kernel, alpha, d_expert),
        grid=(n // tm, MOE_STEPS),
        in_specs=[pl.BlockSpec((tm, d), lambda i, j: (i, 0)),
                  pl.BlockSpec((1, tm, MOE_EXPERTS_PER_STEP), lambda i, j: (j, i, 0)),
                  pl.BlockSpec((None, d, gw), lambda i, j: (layer, 0, j)),
                  pl.BlockSpec((None, d, gw), lambda i, j: (layer, 0, j)),
                  pl.BlockSpec((None, gw, d), lambda i, j: (layer, j, 0)),
                  _layer_spec(ln_g, layer), _layer_spec(ln_b, layer)],
        out_specs=pl.BlockSpec((tm, d), lambda i, j: (i, 0)),
        out_shape=jax.ShapeDtypeStruct((n, d), F32),
        scratch_shapes=[pltpu.VMEM((tm, d), BF16), pltpu.VMEM((tm, d), F32)],
        compiler_params=_cparams(2),
        name="moe_ln",
    )(h, gates_g, wg, wu, wd, ln_g, ln_b)


def _post_block(x, att, ys, yq, lw, layer, alpha, tm):
    n = x.shape[0]
    h, gates_t = _outproj(x, att, ys, yq, lw["wo"], lw["ln1_g"], lw["ln1_b"], lw["wr_t"], lw["br"], layer,
                          alpha, tm)
    gates_g = gates_t.reshape(MOE_STEPS, MOE_EXPERTS_PER_STEP, n).transpose(0, 2, 1)
    return _moe(h, gates_g, lw["wg"], lw["wu"], lw["wd"], lw["ln2_g"], lw["ln2_b"], layer, alpha, tm)


def kernel(x_prompt, x_sample, cache_k, cache_v, cache_logf, page_table, state_ssm_re, state_ssm_im, state_pool, meta_tokens, ln0_g, ln0_b, w_in, b_forget, ssm_lam_re, ssm_lam_im, ssm_log_dt, ssm_b_re, ssm_b_im, ssm_c_re, ssm_c_im, ssm_d, w_glu, b_glu, w_pool, s_pool, w_out, ln1_g, ln1_b, w_router, b_router, w_gate, w_up, w_down, ln2_g, ln2_b):
    bp, seq, d_model = x_prompt.shape
    bs, t_new, _ = x_sample.shape
    depth = w_in.shape[0]
    n_pages, page = page_table.shape[1], cache_k.shape[2]
    past_len = n_pages * page
    length = N_META + seq
    lp = _round_up(length, LANES)
    tile = _pick_tile(lp, (640, 512, 384, 256, 128))
    alpha = (2 * depth) ** 0.25
    n_s = bs * t_new
    tm_s = _pick_tile(n_s, (512, 256, 128, 64, 32, 16, 8))
    assert t_new == SUBLANES and page == LANES and past_len >= POOL_BUF
    assert bs % DECODE_SEQS_PER_STEP == 0

    o1, o2, o3 = D_ATT, 2 * D_ATT, 3 * D_ATT
    o4 = o3 + N_HEADS
    row2 = lambda v: v.reshape(1, -1)

    kt_pages = cache_k.transpose(0, 1, 3, 4, 2).reshape(-1, depth, D_ATT, page)
    vt_pages = cache_v.transpose(0, 1, 3, 4, 2).reshape(-1, depth, D_ATT, page)
    lf_pages = cache_logf.transpose(0, 1, 3, 2)

    meta = jnp.broadcast_to(meta_tokens[None], (bp, N_META, d_model))
    xp = jnp.pad(x_prompt, ((0, 0), (N_META, lp - length), (0, 0)))
    xp = lax.dynamic_update_slice(xp, meta, (0, 0, 0))
    xs = x_sample.reshape(n_s, d_model)
    wr_hi = w_router.T.astype(BF16)
    wr_lo = (w_router.T - wr_hi.astype(F32)).astype(BF16)
    wr_t = jnp.concatenate([wr_hi, wr_lo], axis=0)
    br = b_router.reshape(N_EXPERTS, 1)
    nb_s = _pick_tile(bs, (16, 8, 4, 2, 1))

    outs_p = {k: [] for k in ("k", "v", "lf", "sr", "si", "pb")}
    outs_s = {k: [] for k in ("k", "v", "lf", "sr", "si", "pb")}
    rows3 = lambda v: v.reshape(depth, 1, -1)
    wb = w_in.astype(BF16)
    wq = wb[:, :, :o1]
    wup = wb[:, :, o4:]
    wt = jnp.concatenate([wb[:, :, o1:o3], wb[:, :, o3:o4]], axis=2).transpose(0, 2, 1)
    w_nat = jnp.concatenate([wb[:, :, :o3], wb[:, :, o4:]], axis=2)
    wf_t = wb[:, :, o3:o4].transpose(0, 2, 1)
    bf = b_forget.reshape(depth, N_HEADS, 1)

    apow_re, apow_im, bbr_t, bbi_t = _s5_params(ssm_lam_re, ssm_lam_im, ssm_log_dt, ssm_b_re, ssm_b_im)
    apr = apow_re.reshape(depth, SUBLANES, D_STATE)
    api = apow_im.reshape(depth, SUBLANES, D_STATE)
    ridx = jnp.arange(SUBLANES)[None, :, None]
    ak = []
    for k in (1, 2, 4):
        ak.append(jnp.where(ridx >= k, apr[:, k - 1][:, None], 0.0))
        ak.append(jnp.where(ridx >= k, api[:, k - 1][:, None], 0.0))
    ak += [apr, api]
    mix_w = dict(
        bbr=_block_diag(bbr_t).astype(BF16), bbi=_block_diag(bbi_t).astype(BF16),
        cr=_block_diag(ssm_c_re.transpose(0, 1, 3, 2)).astype(BF16),
        ci=_block_diag(ssm_c_im.transpose(0, 1, 3, 2)).astype(BF16),
        d=rows3(ssm_d), ak=jnp.stack(ak, axis=1), wglu=w_glu.astype(BF16), bglu=rows3(b_glu),
        wpool=_block_diag(w_pool).astype(BF16), spool=rows3(s_pool))
    d_expert = w_gate.shape[-1]
    lw = dict(
        wo=w_out.astype(BF16), ln1_g=rows3(ln1_g), ln1_b=rows3(ln1_b), wr_t=wr_t, br=br,
        wg=w_gate.transpose(0, 2, 1, 3).reshape(depth, d_model, N_EXPERTS * d_expert).astype(BF16),
        wu=w_up.transpose(0, 2, 1, 3).reshape(depth, d_model, N_EXPERTS * d_expert).astype(BF16),
        wd=w_down.reshape(depth, N_EXPERTS * d_expert, d_model).astype(BF16),
        ln2_g=rows3(ln2_g), ln2_b=rows3(ln2_b))

    for l in range(depth):
        res = _inproj_prompt(xp, row2(ln0_g), row2(ln0_b), wq, wup, wt, bf, l, l == 0, tile)
        if l == 0:
            xp, *res = res
        q, kt, vt, ka, vb, lft, u, p = res
        att = _attn_prompt(q, ka, vb, tile)
        ys, yq, sr, si = _seqmix(u, p, mix_w, l, length, tile)
        xp = _post_block(xp.reshape(bp * lp, d_model), att.reshape(bp * lp, D_ATT),
                         ys.reshape(bp * lp, D_SSM), yq.reshape(bp * lp, D_POOL), lw, l, alpha,
                         tile).reshape(bp, lp, d_model)
        outs_p["k"].append(kt[:, :, :length])
        outs_p["v"].append(vt[:, :, :length])
        outs_p["lf"].append(lft[:, :, :length])
        outs_p["sr"].append(sr.reshape(bp, N_SSM_GROUPS, SSM_STATE))
        outs_p["si"].append(si.reshape(bp, N_SSM_GROUPS, SSM_STATE))
        outs_p["pb"].append(p[:, length - POOL_BUF:length])

        res = _inproj_sample(xs, row2(ln0_g), row2(ln0_b), w_nat, wf_t, bf, l, l == 0, tm_s)
        if l == 0:
            xs, *res = res
        qkv, up, lfs_t = res
        lf_new = lfs_t.reshape(N_HEADS, bs, t_new).transpose(1, 0, 2)
        lf_new_pad = jnp.pad(lf_new, ((0, 0), (0, 0), (0, LANES - t_new)))
        att = _attn_decode(page_table, qkv.reshape(bs, t_new, 3 * D_ATT), lf_new_pad,
                           kt_pages, vt_pages, lf_pages, l)
        hist = jnp.concatenate([jnp.zeros((bs, 1, D_POOL), F32), state_pool[:, l]], axis=1)
        up3 = up.reshape(bs, t_new, D_SSM + D_POOL)
        p_s = up3[..., D_SSM:]
        ys, yq, sr, si = _seqmix_sample(up3, state_ssm_re[:, l].reshape(bs, 1, D_STATE),
                                        state_ssm_im[:, l].reshape(bs, 1, D_STATE), hist, mix_w, l, past_len,
                                        nb_s)
        xs = _post_block(xs, att.reshape(n_s, D_ATT), ys.reshape(n_s, D_SSM), yq.reshape(n_s, D_POOL),
                         lw, l, alpha, tm_s)
        qkv3 = qkv.reshape(bs, t_new, 3, N_HEADS, HEAD_DIM)
        outs_s["k"].append(qkv3[:, :, 1])
        outs_s["v"].append(qkv3[:, :, 2])
        outs_s["lf"].append(lf_new.transpose(0, 2, 1))
        outs_s["sr"].append(sr.reshape(bs, N_SSM_GROUPS, SSM_STATE))
        outs_s["si"].append(si.reshape(bs, N_SSM_GROUPS, SSM_STATE))
        outs_s["pb"].append(jnp.concatenate([state_pool[:, l], p_s], axis=1)[:, -POOL_BUF:])

    def heads_last(xs_t):
        a = jnp.stack(xs_t, axis=1)
        b_, dep, _, ln = a.shape
        return a.reshape(b_, dep, N_HEADS, HEAD_DIM, ln).transpose(0, 1, 4, 2, 3)

    y_prompt = xp[:, N_META:length]
    y_sample = xs.reshape(bs, t_new, d_model)
    return (y_prompt, y_sample,
            heads_last(outs_p["k"]), heads_last(outs_p["v"]),
            jnp.stack(outs_p["lf"], axis=1).transpose(0, 1, 3, 2),
            jnp.stack(outs_p["sr"], axis=1), jnp.stack(outs_p["si"], axis=1), jnp.stack(outs_p["pb"], axis=1),
            jnp.stack(outs_s["k"], axis=1), jnp.stack(outs_s["v"], axis=1), jnp.stack(outs_s["lf"], axis=1),
            jnp.stack(outs_s["sr"], axis=1), jnp.stack(outs_s["si"], axis=1), jnp.stack(outs_s["pb"], axis=1))
```
